```python
import math
import jax, jax.numpy as jnp
from jax import lax
import numpy as np

D_MODEL = 2048
BATCH = 16
SEQ = 2048
DEPTH = 2

D_PLE = 256
D_MIX = D_MODEL
D_SGU = D_MIX // 2
D_SSM = D_MIX - D_SGU
SGU_CHUNK = 128
SGU_HEAD = 128
SGU_HEADS = D_SGU // SGU_HEAD
SSM_GROUP = 16
SSM_GROUPS = D_SSM // SSM_GROUP
SSM_STATE = 64
D_FF = 5632
CONV_W = 3
EPS = 1e-6
DT_MIN = 1e-3
DT_MAX = 1e-1

kernel_name = "hybrid_sgu_s5_convffn_ple"


def rmsnorm(x, g):
    x32 = x.astype(jnp.float32)
    y = x32 * lax.rsqrt(jnp.mean(x32 * x32, axis=-1, keepdims=True) + EPS)
    return (y * g.astype(jnp.float32)).astype(x.dtype)


def layernorm(x, g):
    x32 = x.astype(jnp.float32)
    xc = x32 - jnp.mean(x32, axis=-1, keepdims=True)
    y = xc * lax.rsqrt(jnp.mean(xc * xc, axis=-1, keepdims=True) + EPS)
    return (y * g.astype(jnp.float32)).astype(x.dtype)


def spatial_gating(u, v, g, w_s, b_s):
    bsz, seq, _ = u.shape
    v = layernorm(v, g)
    v = v.reshape(bsz, seq // SGU_CHUNK, SGU_CHUNK, SGU_HEADS, SGU_HEAD)
    mask = jnp.tril(jnp.ones((SGU_CHUNK, SGU_CHUNK), dtype=w_s.dtype))
    v = jnp.einsum('hts,bnshc->bnthc', w_s * mask[None], v)
    v = v + jnp.transpose(b_s)[None, None, :, :, None]
    return u * v.reshape(bsz, seq, D_SGU)


def cmul(ar, ai, br, bi):
    return ar * br - ai * bi, ar * bi + ai * br


def s5_combine(left, right):
    al_r, al_i, bl_r, bl_i = left
    ar_r, ar_i, br_r, br_i = right
    a_r, a_i = cmul(ar_r, ar_i, al_r, al_i)
    ab_r, ab_i = cmul(ar_r, ar_i, bl_r, bl_i)
    return a_r, a_i, ab_r + br_r, ab_i + br_i


def s5_mixer(xs, lam_re, lam_im, log_dt, b_re, b_im, c_re, c_im, d, glu_w, glu_b):
    f32 = jnp.float32
    dtype = xs.dtype
    bsz, seq, _ = xs.shape
    x32 = xs.astype(f32)
    xg = x32.reshape(bsz, seq, SSM_GROUPS, SSM_GROUP)
    lr = lam_re.astype(f32)
    li = lam_im.astype(f32)
    dt = jnp.exp(log_dt.astype(f32))[:, None]
    mag = jnp.exp(lr * dt)
    ab_r = mag * jnp.cos(li * dt)
    ab_i = mag * jnp.sin(li * dt)
    den = lr * lr + li * li
    nr = ab_r - 1.0
    q_r = (nr * lr + ab_i * li) / den
    q_i = (ab_i * lr - nr * li) / den
    b_re32 = b_re.astype(f32)
    b_im32 = b_im.astype(f32)
    bb_r = q_r[..., None] * b_re32 - q_i[..., None] * b_im32
    bb_i = q_r[..., None] * b_im32 + q_i[..., None] * b_re32
    bu_r = jnp.einsum('gnc,bsgc->bsgn', bb_r, xg)
    bu_i = jnp.einsum('gnc,bsgc->bsgn', bb_i, xg)
    a_r = jnp.broadcast_to(ab_r, bu_r.shape)
    a_i = jnp.broadcast_to(ab_i, bu_i.shape)
    _, _, s_r, s_i = lax.associative_scan(s5_combine, (a_r, a_i, bu_r, bu_i), axis=1)
    y = (jnp.einsum('gcn,bsgn->bsgc', c_re.astype(f32), s_r)
         - jnp.einsum('gcn,bsgn->bsgc', c_im.astype(f32), s_i))
    y = y.reshape(bsz, seq, D_SSM) + d.astype(f32) * x32
    z = jax.nn.gelu(y)
    z = z * jax.nn.sigmoid(z @ glu_w.astype(f32) + glu_b.astype(f32))
    return z.astype(dtype)


def conv_ffn(h, w_up, conv_w, conv_b, w_down):
    a = h @ w_up
    seq = a.shape[1]
    ap = jnp.pad(a, ((0, 0), (CONV_W - 1, 0), (0, 0)))
    c = (conv_w[0] * ap[:, 0:seq] + conv_w[1] * ap[:, 1:seq + 1]
         + conv_w[2] * ap[:, 2:seq + 2] + conv_b)
    gate, up = jnp.split(c, 2, axis=-1)
    return (jax.nn.silu(gate) * up) @ w_down


def setup_inputs(seed: int = 0) -> dict:
    key = jax.random.key(seed)
    ks = iter(jax.random.split(key, 40))
    nrm = lambda shape, s: jax.random.normal(next(ks), shape, jnp.float32) * s
    gain = lambda shape: 1.0 + nrm(shape, 0.05)
    x = jax.random.normal(next(ks), (BATCH, SEQ, D_MODEL), jnp.float32)
    p = jax.random.normal(next(ks), (DEPTH, BATCH, SEQ, D_PLE), jnp.float32)
    n_idx = jnp.arange(SSM_STATE, dtype=jnp.float32)
    lam_re = -0.5 + nrm((DEPTH, SSM_GROUPS, SSM_STATE), 0.01)
    lam_im = math.pi * n_idx + nrm((DEPTH, SSM_GROUPS, SSM_STATE), 0.01)
    log_dt = jax.random.uniform(next(ks), (DEPTH, SSM_GROUPS), jnp.float32,
                                math.log(DT_MIN), math.log(DT_MAX))
    return {
        "x": x,
        "p": p,
        "mix_norm": gain((DEPTH, D_MODEL)),
        "w_in": nrm((DEPTH, D_MODEL, D_SGU * 2 + D_SSM), D_MODEL ** -0.5),
        "sgu_norm": gain((DEPTH, D_SGU)),
        "sgu_w": nrm((DEPTH, SGU_HEADS, SGU_CHUNK, SGU_CHUNK), SGU_CHUNK ** -0.5),
        "sgu_b": 1.0 + nrm((DEPTH, SGU_HEADS, SGU_CHUNK), 0.1),
        "s5_lam_re": lam_re,
        "s5_lam_im": lam_im,
        "s5_log_dt": log_dt,
        "s5_b_re": nrm((DEPTH, SSM_GROUPS, SSM_STATE, SSM_GROUP), (2 * SSM_GROUP) ** -0.5),
        "s5_b_im": nrm((DEPTH, SSM_GROUPS, SSM_STATE, SSM_GROUP), (2 * SSM_GROUP) ** -0.5),
        "s5_c_re": nrm((DEPTH, SSM_GROUPS, SSM_GROUP, SSM_STATE), SSM_STATE ** -0.5),
        "s5_c_im": nrm((DEPTH, SSM_GROUPS, SSM_GROUP, SSM_STATE), SSM_STATE ** -0.5),
        "s5_d": nrm((DEPTH, D_SSM), 1.0),
        "s5_glu_w": nrm((DEPTH, D_SSM, D_SSM), D_SSM ** -0.5),
        "s5_glu_b": nrm((DEPTH, D_SSM), 0.02),
        "out_norm_a": gain((DEPTH, D_SGU)),
        "out_norm_b": gain((DEPTH, D_SSM)),
        "w_out": nrm((DEPTH, D_MIX, D_MODEL), D_MIX ** -0.5),
        "ffn_norm": gain((DEPTH, D_MODEL)),
        "ffn_w_up": nrm((DEPTH, D_MODEL, 2 * D_FF), D_MODEL ** -0.5),
        "ffn_conv_w": nrm((DEPTH, CONV_W, 2 * D_FF), CONV_W ** -0.5),
        "ffn_conv_b": nrm((DEPTH, 2 * D_FF), 0.02),
        "ffn_w_down": nrm((DEPTH, D_FF, D_MODEL), D_FF ** -0.5),
        "ple_norm": gain((DEPTH, D_MODEL)),
        "ple_w_gate": nrm((DEPTH, D_MODEL, D_MODEL), D_MODEL ** -0.5),
        "ple_w_proj": nrm((DEPTH, D_PLE, D_MODEL), D_PLE ** -0.5),
        "final_norm": gain((D_MODEL,)),
    }


def reference(x, p, mix_norm, w_in, sgu_norm, sgu_w, sgu_b, s5_lam_re, s5_lam_im, s5_log_dt,
              s5_b_re, s5_b_im, s5_c_re, s5_c_im, s5_d, s5_glu_w, s5_glu_b,
              out_norm_a, out_norm_b, w_out, ffn_norm, ffn_w_up, ffn_conv_w, ffn_conv_b,
              ffn_w_down, ple_norm, ple_w_gate, ple_w_proj, final_norm):
    for i in range(DEPTH):
        h = rmsnorm(x, mix_norm[i])
        z = h @ w_in[i]
        u = jax.nn.gelu(z[..., :D_SGU])
        v = jax.nn.gelu(z[..., D_SGU:2 * D_SGU])
        xs = z[..., 2 * D_SGU:]
        ya = spatial_gating(u, v, sgu_norm[i], sgu_w[i], sgu_b[i])
        yb = s5_mixer(xs, s5_lam_re[i], s5_lam_im[i], s5_log_dt[i], s5_b_re[i], s5_b_im[i],
                      s5_c_re[i], s5_c_im[i], s5_d[i], s5_glu_w[i], s5_glu_b[i])
        mix = jnp.concatenate([rmsnorm(ya, out_norm_a[i]), rmsnorm(yb, out_norm_b[i])], axis=-1)
        x = x + mix @ w_out[i]
        x = x + conv_ffn(rmsnorm(x, ffn_norm[i]), ffn_w_up[i], ffn_conv_w[i], ffn_conv_b[i], ffn_w_down[i])
        gate = jax.nn.sigmoid(rmsnorm(x, ple_norm[i]) @ ple_w_gate[i])
        x = x + gate * (p[i] @ ple_w_proj[i])
    return rmsnorm(x, final_norm)
```

```python
import functools
import math

import jax
import jax.numpy as jnp
from jax import lax
from jax.experimental import pallas as pl
from jax.experimental.pallas import tpu as pltpu

EPS = 1e-6
SGU_CHUNK = 128
SGU_HEAD = 128
SSM_GROUP = 16
CONV_W = 3
SSM_BLOCK_GROUPS = 16
HALO = 16
V7X_VMEM_LIMIT_BYTES = 56 * 1024 * 1024

F32 = jnp.float32
BF16 = jnp.bfloat16


def _dot(a, b):
    return jnp.dot(a, b, preferred_element_type=F32)


def _rms(x, g):
    return x * lax.rsqrt(jnp.mean(x * x, axis=-1, keepdims=True) + EPS) * g


def _resident(shape):
    nd = len(shape)
    return pl.BlockSpec(shape, lambda *_: (0,) * nd, pipeline_mode=pl.Buffered(1))


def _params(semantics):
    return pltpu.CompilerParams(dimension_semantics=semantics,
                                vmem_limit_bytes=V7X_VMEM_LIMIT_BYTES)


def _mix_in_kernel(x_ref, gmix_ref, win_ref, gsgu_ref, ws_ref, bs_ref, ga_ref,
                   ya_ref, xs_ref, u_ref, vn_ref, *, d_sgu, n_heads):
    tm = x_ref.shape[0]
    h = _rms(x_ref[...], gmix_ref[...]).astype(BF16)
    u_ref[...] = jax.nn.gelu(_dot(h, win_ref[:, :d_sgu]))
    v = jax.nn.gelu(_dot(h, win_ref[:, d_sgu:2 * d_sgu]))
    xs_ref[...] = _dot(h, win_ref[:, 2 * d_sgu:])
    vc = v - jnp.mean(v, axis=-1, keepdims=True)
    vn = vc * lax.rsqrt(jnp.mean(vc * vc, axis=-1, keepdims=True) + EPS) * gsgu_ref[...]
    vn_ref[...] = vn.astype(BF16)
    for c in range(tm // SGU_CHUNK):
        rows = pl.ds(c * SGU_CHUNK, SGU_CHUNK)
        for hd in range(n_heads):
            cols = pl.ds(hd * SGU_HEAD, SGU_HEAD)
            sv = _dot(ws_ref[hd], vn_ref[rows, cols]) + bs_ref[:, cols]
            u_ref[rows, cols] = u_ref[rows, cols] * sv
    ya_ref[...] = _rms(u_ref[...], ga_ref[...]).astype(BF16)


def _mix_in(x2, gmix, win, gsgu, ws, bs_full, ga, *, tm):
    m, d = x2.shape
    d_sgu = gsgu.shape[-1]
    d_ssm = win.shape[1] - 2 * d_sgu
    n_heads = d_sgu // SGU_HEAD
    kern = functools.partial(_mix_in_kernel, d_sgu=d_sgu, n_heads=n_heads)
    return pl.pallas_call(
        kern,
        grid=(m // tm,),
        in_specs=[
            pl.BlockSpec((tm, d), lambda i: (i, 0)),
            _resident((1, d)),
            _resident(win.shape),
            _resident((1, d_sgu)),
            _resident(ws.shape),
            _resident(bs_full.shape),
            _resident((1, d_sgu)),
        ],
        out_specs=[
            pl.BlockSpec((tm, d_sgu), lambda i: (i, 0)),
            pl.BlockSpec((tm, d_ssm), lambda i: (i, 0)),
        ],
        out_shape=[
            jax.ShapeDtypeStruct((m, d_sgu), BF16),
            jax.ShapeDtypeStruct((m, d_ssm), F32),
        ],
        scratch_shapes=[
            pltpu.VMEM((tm, d_sgu), F32),
            pltpu.VMEM((tm, d_sgu), BF16),
        ],
        compiler_params=_params(("parallel",)),
        name="mix_in",
    )(x2, gmix, win, gsgu, ws, bs_full, ga)


def _split3(v):
    hi = v.astype(BF16)
    r1 = v - hi.astype(F32)
    mid = r1.astype(BF16)
    lo = (r1 - mid.astype(F32)).astype(BF16)
    return hi, mid, lo


def _s5_kernel(xs_ref, perm_ref, permt_ref, bm_ref, cm_ref, ar_ref, ai_ref, d_ref,
               gluw_ref, glub_ref, gb_ref, out_ref,
               h_ref, bu_ref, xtb_ref, ytb_ref, *, lane_w, unroll):
    nb_batch, tc, d_ssm = xs_ref.shape
    rows = nb_batch * tc
    n_blocks, blk_ch, two_n = bm_ref.shape
    n_state = two_n // 2

    @pl.when(pl.program_id(0) == 0)
    def _():
        h_ref[...] = jnp.zeros_like(h_ref)

    xs = xs_ref[...].reshape(rows, d_ssm)
    xtb_ref[...] = _dot(perm_ref[...], xs.astype(BF16)).astype(BF16)

    for blk in range(n_blocks):
        ch = pl.ds(blk * blk_ch, blk_ch)
        bu_ref[...] = _dot(xtb_ref[:, ch], bm_ref[blk])
        for q in range(n_state // lane_w):
            re = pl.ds(q * lane_w, lane_w)
            im = pl.ds(n_state + q * lane_w, lane_w)
            a_r = jnp.broadcast_to(ar_ref[blk, :, re], (nb_batch, lane_w))
            a_i = jnp.broadcast_to(ai_ref[blk, :, re], (nb_batch, lane_w))

            def step(t, carry, re=re, im=im, a_r=a_r, a_i=a_i):
                h_r, h_i = carry
                r = pl.ds(pl.multiple_of(t * nb_batch, nb_batch), nb_batch)
                n_r = a_r * h_r - a_i * h_i + bu_ref[r, re]
                n_i = a_r * h_i + a_i * h_r + bu_ref[r, im]
                bu_ref[r, re] = n_r
                bu_ref[r, im] = n_i
                return n_r, n_i

            h_r, h_i = lax.fori_loop(0, tc, step, (h_ref[blk, :, re], h_ref[blk, :, im]),
                                     unroll=unroll)
            h_ref[blk, :, re] = h_r
            h_ref[blk, :, im] = h_i
        ytb_ref[:, ch] = _dot(bu_ref[...].astype(BF16), cm_ref[blk])

    hi, mid, lo = _split3(ytb_ref[...])
    pt = permt_ref[...]
    y = _dot(pt, hi) + _dot(pt, mid) + _dot(pt, lo)
    y = y + d_ref[...] * xs
    z = jax.nn.gelu(y)
    z = z * jax.nn.sigmoid(_dot(z.astype(BF16), gluw_ref[...]) + glub_ref[...])
    out_ref[...] = _rms(z, gb_ref[...]).astype(BF16).reshape(nb_batch, tc, d_ssm)


def _s5(xs3, perm, permt, bm, cm, a_r, a_i, dskip, gluw, glub, gb, *, tc):
    nb_batch, seq, d_ssm = xs3.shape
    rows = nb_batch * tc
    n_blocks, _, two_n = bm.shape
    kern = functools.partial(_s5_kernel, lane_w=256, unroll=4)
    return pl.pallas_call(
        kern,
        grid=(seq // tc,),
        in_specs=[
            pl.BlockSpec((nb_batch, tc, d_ssm), lambda i: (0, i, 0)),
            _resident(perm.shape),
            _resident(permt.shape),
            _resident(bm.shape),
            _resident(cm.shape),
            _resident(a_r.shape),
            _resident(a_i.shape),
            _resident((1, d_ssm)),
            _resident(gluw.shape),
            _resident((1, d_ssm)),
            _resident((1, d_ssm)),
        ],
        out_specs=pl.BlockSpec((nb_batch, tc, d_ssm), lambda i: (0, i, 0)),
        out_shape=jax.ShapeDtypeStruct((nb_batch, seq, d_ssm), BF16),
        scratch_shapes=[
            pltpu.VMEM((n_blocks, nb_batch, two_n), F32),
            pltpu.VMEM((rows, two_n), F32),
            pltpu.VMEM((rows, d_ssm), BF16),
            pltpu.VMEM((rows, d_ssm), F32),
        ],
        compiler_params=_params(("arbitrary",)),
        name="s5",
    )(xs3, perm, permt, bm, cm, a_r, a_i, dskip, gluw, glub, gb)


def _mix_out_kernel(x_ref, ya_ref, yb_ref, wa_ref, wb_ref, o_ref):
    o_ref[...] = x_ref[...] + _dot(ya_ref[...], wa_ref[...]) + _dot(yb_ref[...], wb_ref[...])


def _mix_out(x2, ya, yb, wa, wb, *, tm):
    m, d = x2.shape
    return pl.pallas_call(
        _mix_out_kernel,
        grid=(m // tm,),
        in_specs=[
            pl.BlockSpec((tm, d), lambda i: (i, 0)),
            pl.BlockSpec((tm, ya.shape[1]), lambda i: (i, 0)),
            pl.BlockSpec((tm, yb.shape[1]), lambda i: (i, 0)),
            _resident(wa.shape),
            _resident(wb.shape),
        ],
        out_specs=pl.BlockSpec((tm, d), lambda i: (i, 0)),
        out_shape=jax.ShapeDtypeStruct((m, d), F32),
        compiler_params=_params(("parallel",)),
        name="mix_out",
    )(x2, ya, yb, wa, wb)


def _ffn_kernel(x_ref, xh_ref, g_ref, wg_ref, wu_ref, cwg_ref, cwu_ref, cbg_ref, cbu_ref,
                wd_ref, o_ref, hs_ref, *, tiles_per_seq):
    i = pl.program_id(0)
    f = pl.program_id(1)
    tm = x_ref.shape[0]

    @pl.when(f == 0)
    def _():
        x = x_ref[...]
        o_ref[...] = x
        hs_ref[pl.ds(HALO, tm), :] = _rms(x, g_ref[...]).astype(BF16)
        keep = (i % tiles_per_seq != 0).astype(F32)
        hs_ref[pl.ds(0, HALO), :] = (_rms(xh_ref[...], g_ref[...]) * keep).astype(BF16)

    hs = hs_ref[...]

    def conv(a, cw_ref, cb_ref):
        return (cw_ref[0:1, :] * a[HALO - 2:HALO - 2 + tm]
                + cw_ref[1:2, :] * a[HALO - 1:HALO - 1 + tm]
                + cw_ref[2:3, :] * a[HALO:HALO + tm] + cb_ref[...])

    gate = conv(_dot(hs, wg_ref[...]), cwg_ref, cbg_ref)
    up = conv(_dot(hs, wu_ref[...]), cwu_ref, cbu_ref)
    act = (jax.nn.silu(gate) * up).astype(BF16)
    o_ref[...] += _dot(act, wd_ref[...])


def _ffn(x2, g, w_up, conv_w, conv_b, w_down, *, tm, tf, seq):
    m, d = x2.shape
    d_ff = w_down.shape[0]
    nf = d_ff // tf
    halo_blocks = tm // HALO
    kern = functools.partial(_ffn_kernel, tiles_per_seq=seq // tm)
    return pl.pallas_call(
        kern,
        grid=(m // tm, nf),
        in_specs=[
            pl.BlockSpec((tm, d), lambda i, f: (i, 0)),
            pl.BlockSpec((HALO, d), lambda i, f: (jnp.maximum(i * halo_blocks - 1, 0), 0)),
            _resident((1, d)),
            pl.BlockSpec((d, tf), lambda i, f: (0, f)),
            pl.BlockSpec((d, tf), lambda i, f: (0, nf + f)),
            pl.BlockSpec((CONV_W, tf), lambda i, f: (0, f)),
            pl.BlockSpec((CONV_W, tf), lambda i, f: (0, nf + f)),
            pl.BlockSpec((1, tf), lambda i, f: (0, f)),
            pl.BlockSpec((1, tf), lambda i, f: (0, nf + f)),
            pl.BlockSpec((tf, d), lambda i, f: (f, 0)),
        ],
        out_specs=pl.BlockSpec((tm, d), lambda i, f: (i, 0)),
        out_shape=jax.ShapeDtypeStruct((m, d), F32),
        scratch_shapes=[pltpu.VMEM((HALO + tm, d), BF16)],
        compiler_params=_params(("parallel", "arbitrary")),
        name="ffn",
    )(x2, x2, g, w_up, w_up, conv_w, conv_w, conv_b, conv_b, w_down)


def _ple_kernel(x_ref, p_ref, g_ref, wg_ref, wp_ref, gf_ref, o_ref, *, final):
    x = x_ref[...]
    gate = jax.nn.sigmoid(_dot(_rms(x, g_ref[...]).astype(BF16), wg_ref[...]))
    y = x + gate * _dot(p_ref[...].astype(BF16), wp_ref[...])
    if final:
        y = _rms(y, gf_ref[...])
    o_ref[...] = y


def _ple(x2, p2, g, wg, wp, gf, *, tm, final):
    m, d = x2.shape
    kern = functools.partial(_ple_kernel, final=final)
    return pl.pallas_call(
        kern,
        grid=(m // tm,),
        in_specs=[
            pl.BlockSpec((tm, d), lambda i: (i, 0)),
            pl.BlockSpec((tm, p2.shape[1]), lambda i: (i, 0)),
            _resident((1, d)),
            _resident(wg.shape),
            _resident(wp.shape),
            _resident((1, d)),
        ],
        out_specs=pl.BlockSpec((tm, d), lambda i: (i, 0)),
        out_shape=jax.ShapeDtypeStruct((m, d), F32),
        compiler_params=_params(("parallel",)),
        name="ple_final" if final else "ple",
    )(x2, p2, g, wg, wp, gf)


def _s5_params(lam_re, lam_im, log_dt, b_re, b_im, c_re, c_im):
    n_groups, n_state = lam_re.shape
    dt = jnp.exp(log_dt)[:, None]
    mag = jnp.exp(lam_re * dt)
    ab_r = mag * jnp.cos(lam_im * dt)
    ab_i = mag * jnp.sin(lam_im * dt)
    den = lam_re * lam_re + lam_im * lam_im
    nr = ab_r - 1.0
    q_r = (nr * lam_re + ab_i * lam_im) / den
    q_i = (ab_i * lam_re - nr * lam_im) / den
    bb_r = q_r[..., None] * b_re - q_i[..., None] * b_im
    bb_i = q_r[..., None] * b_im + q_i[..., None] * b_re
    gb = SSM_BLOCK_GROUPS
    n_blocks = n_groups // gb
    eye = jnp.eye(gb, dtype=F32)

    def pack_b(bb):
        bb = bb.reshape(n_blocks, gb, n_state, SSM_GROUP)
        out = jnp.einsum('kgnc,gh->kgchn', bb, eye)
        return out.reshape(n_blocks, gb * SSM_GROUP, gb * n_state)

    def pack_c(cc):
        cc = cc.reshape(n_blocks, gb, SSM_GROUP, n_state)
        out = jnp.einsum('kgcn,gh->kgnhc', cc, eye)
        return out.reshape(n_blocks, gb * n_state, gb * SSM_GROUP)

    bm = jnp.concatenate([pack_b(bb_r), pack_b(bb_i)], axis=-1).astype(BF16)
    cm = jnp.concatenate([pack_c(c_re), pack_c(-c_im)], axis=1).astype(BF16)
    a_r = ab_r.reshape(n_blocks, 1, gb * n_state)
    a_i = ab_i.reshape(n_blocks, 1, gb * n_state)
    return bm, cm, a_r, a_i


def _row_perm(nb_batch, tc):
    dst = jnp.arange(nb_batch * tc)
    src = (dst % nb_batch) * tc + dst // nb_batch
    return (src[:, None] == jnp.arange(nb_batch * tc)[None, :]).astype(BF16)


def _tiles(m, seq):
    tm = min(512, seq)
    tc = min(32, seq)
    return tm, tc


def kernel(x, p, mix_norm, w_in, sgu_norm, sgu_w, sgu_b, s5_lam_re, s5_lam_im, s5_log_dt,
           s5_b_re, s5_b_im, s5_c_re, s5_c_im, s5_d, s5_glu_w, s5_glu_b,
           out_norm_a, out_norm_b, w_out, ffn_norm, ffn_w_up, ffn_conv_w, ffn_conv_b,
           ffn_w_down, ple_norm, ple_w_gate, ple_w_proj, final_norm):
    depth = w_in.shape[0]
    nb_batch, seq, d = x.shape
    m = nb_batch * seq
    d_sgu = sgu_norm.shape[-1]
    d_ssm = s5_d.shape[-1]
    d_ff = ffn_w_down.shape[1]
    n_heads = d_sgu // SGU_HEAD
    assert seq % SGU_CHUNK == 0 and d_sgu % SGU_HEAD == 0
    assert s5_lam_re.shape[1] % SSM_BLOCK_GROUPS == 0
    tm, tc = _tiles(m, seq)
    tf = 512 if d_ff % 512 == 0 else d_ff
    assert seq % tm == 0 and tm % SGU_CHUNK == 0 and seq % tc == 0 and tc % 8 == 0

    perm = _row_perm(nb_batch, tc)
    permt = perm.T
    tril = jnp.tril(jnp.ones((SGU_CHUNK, SGU_CHUNK), F32))
    row = lambda v: v.reshape(1, -1)

    x2 = x.reshape(m, d)
    for i in range(depth):
        ws = (sgu_w[i] * tril[None]).astype(BF16)
        bs_full = jnp.repeat(jnp.transpose(sgu_b[i]), SGU_HEAD, axis=1)
        ya, xs = _mix_in(x2, row(mix_norm[i]), w_in[i].astype(BF16), row(sgu_norm[i]), ws, bs_full,
                         row(out_norm_a[i]), tm=tm)
        bm, cm, a_r, a_i = _s5_params(s5_lam_re[i], s5_lam_im[i], s5_log_dt[i], s5_b_re[i], s5_b_im[i],
                                      s5_c_re[i], s5_c_im[i])
        yb = _s5(xs.reshape(nb_batch, seq, d_ssm), perm, permt, bm, cm, a_r, a_i, row(s5_d[i]),
                 s5_glu_w[i].astype(BF16), row(s5_glu_b[i]), row(out_norm_b[i]), tc=tc)
        wo = w_out[i].astype(BF16)
        x2 = _mix_out(x2, ya, yb.reshape(m, d_ssm), wo[:d_sgu], wo[d_sgu:], tm=tm)
        x2 = _ffn(x2, row(ffn_norm[i]), ffn_w_up[i].astype(BF16), ffn_conv_w[i], row(ffn_conv_b[i]),
                  ffn_w_down[i].astype(BF16), tm=tm, tf=tf, seq=seq)
        x2 = _ple(x2, p[i].reshape(m, -1), row(ple_norm[i]), ple_w_gate[i].astype(BF16),
                  ple_w_proj[i].astype(BF16), row(final_norm), tm=tm, final=(i == depth - 1))
    return x2.reshape(nb_batch, seq, d)
```

```python
import functools
import math

import jax
import jax.numpy as jnp
from jax import lax
from jax.experimental import pallas as pl
from jax.experimental.pallas import tpu as pltpu

EPS = 1e-6
SGU_CHUNK = 128
SGU_HEAD = 128
SSM_GROUP = 16
CONV_W = 3
SSM_BLOCK_GROUPS = 16
CARRY = 8
V7X_VMEM_LIMIT_BYTES = 56 * 1024 * 1024

F32 = jnp.float32
BF16 = jnp.bfloat16


def _dot(a, b):
    return jnp.dot(a, b, preferred_element_type=F32)


def _rms(x, g):
    return x * lax.rsqrt(jnp.mean(x * x, axis=-1, keepdims=True) + EPS) * g


def _resident(shape):
    nd = len(shape)
    return pl.BlockSpec(shape, lambda *_: (0,) * nd, pipeline_mode=pl.Buffered(1))


def _params(semantics):
    return pltpu.CompilerParams(dimension_semantics=semantics,
                                vmem_limit_bytes=V7X_VMEM_LIMIT_BYTES)


def _mix_in_kernel(x_ref, gmix_ref, win_ref, gsgu_ref, ws_ref, bs_ref, ga_ref,
                   ya_ref, xs_ref, u_ref, vn_ref, *, d_sgu, n_heads):
    tm = x_ref.shape[0]
    h = _rms(x_ref[...], gmix_ref[...]).astype(BF16)
    u_ref[...] = jax.nn.gelu(_dot(h, win_ref[:, :d_sgu]))
    v = jax.nn.gelu(_dot(h, win_ref[:, d_sgu:2 * d_sgu]))
    xs_ref[...] = _dot(h, win_ref[:, 2 * d_sgu:])
    vc = v - jnp.mean(v, axis=-1, keepdims=True)
    vn = vc * lax.rsqrt(jnp.mean(vc * vc, axis=-1, keepdims=True) + EPS) * gsgu_ref[...]
    vn_ref[...] = vn.astype(BF16)
    for c in range(tm // SGU_CHUNK):
        rows = pl.ds(c * SGU_CHUNK, SGU_CHUNK)
        for hd in range(n_heads):
            cols = pl.ds(hd * SGU_HEAD, SGU_HEAD)
            sv = _dot(ws_ref[hd], vn_ref[rows, cols]) + bs_ref[:, cols]
            u_ref[rows, cols] = u_ref[rows, cols] * sv
    ya_ref[...] = _rms(u_ref[...], ga_ref[...]).astype(BF16)


def _mix_in(x2, gmix, win, gsgu, ws, bs_full, ga, *, tm):
    m, d = x2.shape
    d_sgu = gsgu.shape[-1]
    d_ssm = win.shape[1] - 2 * d_sgu
    n_heads = d_sgu // SGU_HEAD
    kern = functools.partial(_mix_in_kernel, d_sgu=d_sgu, n_heads=n_heads)
    return pl.pallas_call(
        kern,
        grid=(m // tm,),
        in_specs=[
            pl.BlockSpec((tm, d), lambda i: (i, 0)),
            _resident((1, d)),
            _resident(win.shape),
            _resident((1, d_sgu)),
            _resident(ws.shape),
            _resident(bs_full.shape),
            _resident((1, d_sgu)),
        ],
        out_specs=[
            pl.BlockSpec((tm, d_sgu), lambda i: (i, 0)),
            pl.BlockSpec((tm, d_ssm), lambda i: (i, 0)),
        ],
        out_shape=[
            jax.ShapeDtypeStruct((m, d_sgu), BF16),
            jax.ShapeDtypeStruct((m, d_ssm), F32),
        ],
        scratch_shapes=[
            pltpu.VMEM((tm, d_sgu), F32),
            pltpu.VMEM((tm, d_sgu), BF16),
        ],
        compiler_params=_params(("parallel",)),
        name="mix_in",
    )(x2, gmix, win, gsgu, ws, bs_full, ga)


def _split3(v):
    hi = v.astype(BF16)
    r1 = v - hi.astype(F32)
    mid = r1.astype(BF16)
    lo = (r1 - mid.astype(F32)).astype(BF16)
    return hi, mid, lo


def _s5_kernel(xs_ref, perm_ref, permt_ref, bm_ref, cm_ref, ar_ref, ai_ref, d_ref,
               gluw_ref, glub_ref, gb_ref, out_ref,
               h_ref, bu_ref, xtb_ref, ytb_ref, *, lane_w, unroll):
    nb_batch, tc, d_ssm = xs_ref.shape
    rows = nb_batch * tc
    n_blocks, blk_ch, two_n = bm_ref.shape
    n_state = two_n // 2

    @pl.when(pl.program_id(0) == 0)
    def _():
        h_ref[...] = jnp.zeros_like(h_ref)

    xs = xs_ref[...].reshape(rows, d_ssm)
    xtb_ref[...] = _dot(perm_ref[...], xs.astype(BF16)).astype(BF16)

    for blk in range(n_blocks):
        ch = pl.ds(blk * blk_ch, blk_ch)
        bu_ref[...] = _dot(xtb_ref[:, ch], bm_ref[blk])
        for q in range(n_state // lane_w):
            re = pl.ds(q * lane_w, lane_w)
            im = pl.ds(n_state + q * lane_w, lane_w)
            a_r = jnp.broadcast_to(ar_ref[blk, :, re], (nb_batch, lane_w))
            a_i = jnp.broadcast_to(ai_ref[blk, :, re], (nb_batch, lane_w))

            def step(t, carry, re=re, im=im, a_r=a_r, a_i=a_i):
                h_r, h_i = carry
                r = pl.ds(pl.multiple_of(t * nb_batch, nb_batch), nb_batch)
                n_r = a_r * h_r - a_i * h_i + bu_ref[r, re]
                n_i = a_r * h_i + a_i * h_r + bu_ref[r, im]
                bu_ref[r, re] = n_r
                bu_ref[r, im] = n_i
                return n_r, n_i

            h_r, h_i = lax.fori_loop(0, tc, step, (h_ref[blk, :, re], h_ref[blk, :, im]),
                                     unroll=unroll)
            h_ref[blk, :, re] = h_r
            h_ref[blk, :, im] = h_i
        ytb_ref[:, ch] = _dot(bu_ref[...].astype(BF16), cm_ref[blk])

    hi, mid, lo = _split3(ytb_ref[...])
    pt = permt_ref[...]
    y = _dot(pt, hi) + _dot(pt, mid) + _dot(pt, lo)
    y = y + d_ref[...] * xs
    z = jax.nn.gelu(y)
    z = z * jax.nn.sigmoid(_dot(z.astype(BF16), gluw_ref[...]) + glub_ref[...])
    out_ref[...] = _rms(z, gb_ref[...]).astype(BF16).reshape(nb_batch, tc, d_ssm)


def _s5(xs3, perm, permt, bm, cm, a_r, a_i, dskip, gluw, glub, gb, *, tc):
    nb_batch, seq, d_ssm = xs3.shape
    rows = nb_batch * tc
    n_blocks, _, two_n = bm.shape
    kern = functools.partial(_s5_kernel, lane_w=256, unroll=4)
    return pl.pallas_call(
        kern,
        grid=(seq // tc,),
        in_specs=[
            pl.BlockSpec((nb_batch, tc, d_ssm), lambda i: (0, i, 0)),
            _resident(perm.shape),
            _resident(permt.shape),
            _resident(bm.shape),
            _resident(cm.shape),
            _resident(a_r.shape),
            _resident(a_i.shape),
            _resident((1, d_ssm)),
            _resident(gluw.shape),
            _resident((1, d_ssm)),
            _resident((1, d_ssm)),
        ],
        out_specs=pl.BlockSpec((nb_batch, tc, d_ssm), lambda i: (0, i, 0)),
        out_shape=jax.ShapeDtypeStruct((nb_batch, seq, d_ssm), BF16),
        scratch_shapes=[
            pltpu.VMEM((n_blocks, nb_batch, two_n), F32),
            pltpu.VMEM((rows, two_n), F32),
            pltpu.VMEM((rows, d_ssm), BF16),
            pltpu.VMEM((rows, d_ssm), F32),
        ],
        compiler_params=_params(("arbitrary",)),
        name="s5",
    )(xs3, perm, permt, bm, cm, a_r, a_i, dskip, gluw, glub, gb)


def _mix_out_kernel(x_ref, ya_ref, yb_ref, wa_ref, wb_ref, g_ref, o_ref, hn_ref):
    x1 = x_ref[...] + _dot(ya_ref[...], wa_ref[...]) + _dot(yb_ref[...], wb_ref[...])
    o_ref[...] = x1
    hn_ref[...] = _rms(x1, g_ref[...]).astype(BF16)


def _mix_out(x2, ya, yb, wa, wb, g, *, tm):
    m, d = x2.shape
    return pl.pallas_call(
        _mix_out_kernel,
        grid=(m // tm,),
        in_specs=[
            pl.BlockSpec((tm, d), lambda i: (i, 0)),
            pl.BlockSpec((tm, ya.shape[1]), lambda i: (i, 0)),
            pl.BlockSpec((tm, yb.shape[1]), lambda i: (i, 0)),
            _resident(wa.shape),
            _resident(wb.shape),
            _resident((1, d)),
        ],
        out_specs=[
            pl.BlockSpec((tm, d), lambda i: (i, 0)),
            pl.BlockSpec((tm, d), lambda i: (i, 0)),
        ],
        out_shape=[
            jax.ShapeDtypeStruct((m, d), F32),
            jax.ShapeDtypeStruct((m, d), BF16),
        ],
        compiler_params=_params(("parallel",)),
        name="mix_out",
    )(x2, ya, yb, wa, wb, g)


def _ffn_kernel(hn_ref, wgu_ref, cw_ref, cb_ref, wd_ref, o_ref, carry_ref, a_ref,
                *, tiles_per_seq, rsub):
    i = pl.program_id(0)
    f = pl.program_id(1)
    tm = hn_ref.shape[0]
    tf = wd_ref.shape[0]
    n_sub = tm // rsub

    @pl.when(f == 0)
    def _():
        o_ref[...] = jnp.zeros_like(o_ref)

    @pl.when(i % tiles_per_seq == 0)
    def _():
        carry_ref[f] = jnp.zeros(carry_ref.shape[1:], F32)

    a_ref[0, pl.ds(0, CARRY), :] = carry_ref[f]

    def project(r):
        a_ref[r % 2, pl.ds(CARRY, rsub), :] = _dot(hn_ref[pl.ds(r * rsub, rsub), :], wgu_ref[...])

    project(0)
    for r in range(n_sub):
        slot = r % 2
        if r + 1 < n_sub:
            project(r + 1)
        c = (cw_ref[0:1, :] * a_ref[slot, pl.ds(CARRY - 2, rsub), :]
             + cw_ref[1:2, :] * a_ref[slot, pl.ds(CARRY - 1, rsub), :]
             + cw_ref[2:3, :] * a_ref[slot, pl.ds(CARRY, rsub), :] + cb_ref[...])
        tail = a_ref[slot, pl.ds(rsub, CARRY), :]
        if r + 1 < n_sub:
            a_ref[1 - slot, pl.ds(0, CARRY), :] = tail
        else:
            carry_ref[f] = tail
        act = (jax.nn.silu(c[:, :tf]) * c[:, tf:]).astype(BF16)
        o_ref[pl.ds(r * rsub, rsub), :] += _dot(act, wd_ref[...])


def _ffn(hn, w_gu, conv_w, conv_b, w_down, *, tm, tf, rsub, seq):
    m, d = hn.shape
    d_ff = w_down.shape[0]
    nf = d_ff // tf
    kern = functools.partial(_ffn_kernel, tiles_per_seq=seq // tm, rsub=rsub)
    return pl.pallas_call(
        kern,
        grid=(m // tm, nf),
        in_specs=[
            pl.BlockSpec((tm, d), lambda i, f: (i, 0)),
            pl.BlockSpec((d, 2 * tf), lambda i, f: (0, f)),
            pl.BlockSpec((CONV_W, 2 * tf), lambda i, f: (0, f)),
            pl.BlockSpec((1, 2 * tf), lambda i, f: (0, f)),
            pl.BlockSpec((tf, d), lambda i, f: (f, 0)),
        ],
        out_specs=pl.BlockSpec((tm, d), lambda i, f: (i, 0)),
        out_shape=jax.ShapeDtypeStruct((m, d), F32),
        scratch_shapes=[
            pltpu.VMEM((nf, CARRY, 2 * tf), F32),
            pltpu.VMEM((2, CARRY + rsub, 2 * tf), F32),
        ],
        compiler_params=_params(("arbitrary", "arbitrary")),
        name="ffn",
    )(hn, w_gu, conv_w, conv_b, w_down)


def _pair_blocks(w, tf):
    lead = w.shape[:-1]
    nf = w.shape[-1] // (2 * tf)
    return jnp.swapaxes(w.reshape(*lead, 2, nf, tf), -3, -2).reshape(*lead, 2 * nf * tf)


def _ple_kernel(x_ref, dl_ref, p_ref, g_ref, wg_ref, wp_ref, gf_ref, o_ref, *, final):
    x = x_ref[...] + dl_ref[...]
    gate = jax.nn.sigmoid(_dot(_rms(x, g_ref[...]).astype(BF16), wg_ref[...]))
    y = x + gate * _dot(p_ref[...].astype(BF16), wp_ref[...])
    if final:
        y = _rms(y, gf_ref[...])
    o_ref[...] = y


def _ple(x2, delta, p2, g, wg, wp, gf, *, tm, final):
    m, d = x2.shape
    kern = functools.partial(_ple_kernel, final=final)
    return pl.pallas_call(
        kern,
        grid=(m // tm,),
        in_specs=[
            pl.BlockSpec((tm, d), lambda i: (i, 0)),
            pl.BlockSpec((tm, d), lambda i: (i, 0)),
            pl.BlockSpec((tm, p2.shape[1]), lambda i: (i, 0)),
            _resident((1, d)),
            _resident(wg.shape),
            _resident(wp.shape),
            _resident((1, d)),
        ],
        out_specs=pl.BlockSpec((tm, d), lambda i: (i, 0)),
        out_shape=jax.ShapeDtypeStruct((m, d), F32),
        compiler_params=_params(("parallel",)),
        name="ple_final" if final else "ple",
    )(x2, delta, p2, g, wg, wp, gf)


def _s5_params(lam_re, lam_im, log_dt, b_re, b_im, c_re, c_im):
    n_groups, n_state = lam_re.shape
    dt = jnp.exp(log_dt)[:, None]
    mag = jnp.exp(lam_re * dt)
    ab_r = mag * jnp.cos(lam_im * dt)
    ab_i = mag * jnp.sin(lam_im * dt)
    den = lam_re * lam_re + lam_im * lam_im
    nr = ab_r - 1.0
    q_r = (nr * lam_re + ab_i * lam_im) / den
    q_i = (ab_i * lam_re - nr * lam_im) / den
    bb_r = q_r[..., None] * b_re - q_i[..., None] * b_im
    bb_i = q_r[..., None] * b_im + q_i[..., None] * b_re
    gb = SSM_BLOCK_GROUPS
    n_blocks = n_groups // gb
    eye = jnp.eye(gb, dtype=F32)

    def pack_b(bb):
        bb = bb.reshape(n_blocks, gb, n_state, SSM_GROUP)
        out = jnp.einsum('kgnc,gh->kgchn', bb, eye)
        return out.reshape(n_blocks, gb * SSM_GROUP, gb * n_state)

    def pack_c(cc):
        cc = cc.reshape(n_blocks, gb, SSM_GROUP, n_state)
        out = jnp.einsum('kgcn,gh->kgnhc', cc, eye)
        return out.reshape(n_blocks, gb * n_state, gb * SSM_GROUP)

    bm = jnp.concatenate([pack_b(bb_r), pack_b(bb_i)], axis=-1).astype(BF16)
    cm = jnp.concatenate([pack_c(c_re), pack_c(-c_im)], axis=1).astype(BF16)
    a_r = ab_r.reshape(n_blocks, 1, gb * n_state)
    a_i = ab_i.reshape(n_blocks, 1, gb * n_state)
    return bm, cm, a_r, a_i


def _row_perm(nb_batch, tc):
    dst = jnp.arange(nb_batch * tc)
    src = (dst % nb_batch) * tc + dst // nb_batch
    return (src[:, None] == jnp.arange(nb_batch * tc)[None, :]).astype(BF16)


def _tiles(m, seq):
    tm = min(512, seq)
    tc = min(32, seq)
    tm_ffn = min(1024, seq)
    rsub = min(256, tm_ffn)
    return tm, tc, tm_ffn, rsub


def kernel(x, p, mix_norm, w_in, sgu_norm, sgu_w, sgu_b, s5_lam_re, s5_lam_im, s5_log_dt,
           s5_b_re, s5_b_im, s5_c_re, s5_c_im, s5_d, s5_glu_w, s5_glu_b,
           out_norm_a, out_norm_b, w_out, ffn_norm, ffn_w_up, ffn_conv_w, ffn_conv_b,
           ffn_w_down, ple_norm, ple_w_gate, ple_w_proj, final_norm):
    depth = w_in.shape[0]
    nb_batch, seq, d = x.shape
    m = nb_batch * seq
    d_sgu = sgu_norm.shape[-1]
    d_ssm = s5_d.shape[-1]
    d_ff = ffn_w_down.shape[1]
    n_heads = d_sgu // SGU_HEAD
    assert seq % SGU_CHUNK == 0 and d_sgu % SGU_HEAD == 0
    assert s5_lam_re.shape[1] % SSM_BLOCK_GROUPS == 0
    tm, tc, tm_ffn, rsub = _tiles(m, seq)
    tf = 512 if d_ff % 512 == 0 else d_ff
    assert seq % tm == 0 and tm % SGU_CHUNK == 0 and seq % tc == 0 and tc % 8 == 0
    assert seq % tm_ffn == 0 and tm_ffn % rsub == 0 and rsub % CARRY == 0

    perm = _row_perm(nb_batch, tc)
    permt = perm.T
    tril = jnp.tril(jnp.ones((SGU_CHUNK, SGU_CHUNK), F32))
    row = lambda v: v.reshape(1, -1)

    x2 = x.reshape(m, d)
    for i in range(depth):
        ws = (sgu_w[i] * tril[None]).astype(BF16)
        bs_full = jnp.repeat(jnp.transpose(sgu_b[i]), SGU_HEAD, axis=1)
        ya, xs = _mix_in(x2, row(mix_norm[i]), w_in[i].astype(BF16), row(sgu_norm[i]), ws, bs_full,
                         row(out_norm_a[i]), tm=tm)
        bm, cm, a_r, a_i = _s5_params(s5_lam_re[i], s5_lam_im[i], s5_log_dt[i], s5_b_re[i], s5_b_im[i],
                                      s5_c_re[i], s5_c_im[i])
        yb = _s5(xs.reshape(nb_batch, seq, d_ssm), perm, permt, bm, cm, a_r, a_i, row(s5_d[i]),
                 s5_glu_w[i].astype(BF16), row(s5_glu_b[i]), row(out_norm_b[i]), tc=tc)
        wo = w_out[i].astype(BF16)
        x2, hn = _mix_out(x2, ya, yb.reshape(m, d_ssm), wo[:d_sgu], wo[d_sgu:], row(ffn_norm[i]), tm=tm)
        delta = _ffn(hn, _pair_blocks(ffn_w_up[i], tf).astype(BF16), _pair_blocks(ffn_conv_w[i], tf),
                     _pair_blocks(row(ffn_conv_b[i]), tf), ffn_w_down[i].astype(BF16),
                     tm=tm_ffn, tf=tf, rsub=rsub, seq=seq)
        x2 = _ple(x2, delta, p[i].reshape(m, -1), row(ple_norm[i]), ple_w_gate[i].astype(BF16),
                  ple_w_proj[i].astype(BF16), row(final_norm), tm=tm, final=(i == depth - 1))
    return x2.reshape(nb_batch, seq, d)
```

```python
import functools
import math

import jax
import jax.numpy as jnp
from jax import lax
from jax.experimental import pallas as pl
from jax.experimental.pallas import tpu as pltpu

EPS = 1e-6
SGU_CHUNK = 128
SGU_HEAD = 128
SSM_GROUP = 16
CONV_W = 3
LANES = 128
SSM_BLOCK_GROUPS = 16
CARRY = 8
V7X_VMEM_LIMIT_BYTES = 56 * 1024 * 1024

F32 = jnp.float32
BF16 = jnp.bfloat16


def _dot(a, b):
    return jnp.dot(a, b, preferred_element_type=F32)


def _rms(x, g):
    return x * lax.rsqrt(jnp.mean(x * x, axis=-1, keepdims=True) + EPS) * g


def _resident(shape):
    nd = len(shape)
    return pl.BlockSpec(shape, lambda *_: (0,) * nd, pipeline_mode=pl.Buffered(1))


def _params(semantics):
    return pltpu.CompilerParams(dimension_semantics=semantics,
                                vmem_limit_bytes=V7X_VMEM_LIMIT_BYTES)


def _mix_in_kernel(x_ref, gmix_ref, win_ref, gsgu_ref, ws_ref, bs_ref, ga_ref,
                   ya_ref, xs_ref, u_ref, vn_ref, *, d_sgu, n_heads):
    tm = x_ref.shape[0]
    h = _rms(x_ref[...], gmix_ref[...]).astype(BF16)
    u_ref[...] = jax.nn.gelu(_dot(h, win_ref[:, :d_sgu]))
    v = jax.nn.gelu(_dot(h, win_ref[:, d_sgu:2 * d_sgu]))
    xs = _dot(h, win_ref[:, 2 * d_sgu:])
    n_tt, n_slabs, tc, _ = xs_ref.shape
    for j in range(n_slabs):
        xs_ref[:, j] = xs[:, j * LANES:(j + 1) * LANES].reshape(n_tt, tc, LANES)
    vc = v - jnp.mean(v, axis=-1, keepdims=True)
    vn = vc * lax.rsqrt(jnp.mean(vc * vc, axis=-1, keepdims=True) + EPS) * gsgu_ref[...]
    vn_ref[...] = vn.astype(BF16)
    for c in range(tm // SGU_CHUNK):
        rows = pl.ds(c * SGU_CHUNK, SGU_CHUNK)
        for hd in range(n_heads):
            cols = pl.ds(hd * SGU_HEAD, SGU_HEAD)
            sv = _dot(ws_ref[hd], vn_ref[rows, cols]) + bs_ref[:, cols]
            u_ref[rows, cols] = u_ref[rows, cols] * sv
    ya_ref[...] = _rms(u_ref[...], ga_ref[...]).astype(BF16)


def _mix_in(x2, gmix, win, gsgu, ws, bs_full, ga, *, tm, tc, seq):
    m, d = x2.shape
    tiles_per_seq = seq // tm
    d_sgu = gsgu.shape[-1]
    n_slabs = (win.shape[1] - 2 * d_sgu) // LANES
    n_heads = d_sgu // SGU_HEAD
    kern = functools.partial(_mix_in_kernel, d_sgu=d_sgu, n_heads=n_heads)
    return pl.pallas_call(
        kern,
        grid=(m // tm,),
        in_specs=[
            pl.BlockSpec((tm, d), lambda i: (i, 0)),
            _resident((1, d)),
            _resident(win.shape),
            _resident((1, d_sgu)),
            _resident(ws.shape),
            _resident(bs_full.shape),
            _resident((1, d_sgu)),
        ],
        out_specs=[
            pl.BlockSpec((tm, d_sgu), lambda i: (i, 0)),
            pl.BlockSpec((tm // tc, n_slabs, tc, LANES),
                         lambda i: (i % tiles_per_seq, 0, i // tiles_per_seq, 0)),
        ],
        out_shape=[
            jax.ShapeDtypeStruct((m, d_sgu), BF16),
            jax.ShapeDtypeStruct((seq // tc, n_slabs, (m // seq) * tc, LANES), F32),
        ],
        scratch_shapes=[
            pltpu.VMEM((tm, d_sgu), F32),
            pltpu.VMEM((tm, d_sgu), BF16),
        ],
        compiler_params=_params(("parallel",)),
        name="mix_in",
    )(x2, gmix, win, gsgu, ws, bs_full, ga)


def _s5_kernel(xs_ref, permt_ref, bm_ref, cm_ref, ar_ref, ai_ref, d_ref,
               gluw_ref, glub_ref, gb_ref, out_ref,
               h_ref, bu_ref, xtb_ref, xbf_ref, ytb_ref, *, lane_w):
    n_slabs, rows, _ = xs_ref.shape
    nb_batch, tc, _ = out_ref.shape
    n_blocks, blk_ch, two_n = bm_ref.shape
    n_state = two_n // 2
    d_ssm = n_slabs * LANES

    @pl.when(pl.program_id(0) == 0)
    def _():
        h_ref[...] = jnp.zeros_like(h_ref)

    for j in range(n_slabs):
        for t in range(tc):
            xtb_ref[pl.ds(t * nb_batch, nb_batch), pl.ds(j * LANES, LANES)] = (
                xs_ref[j, pl.ds(t, nb_batch, stride=tc), :])
    xbf_ref[...] = xtb_ref[...].astype(BF16)

    def drive(blk):
        bu_ref[blk % 2] = _dot(xbf_ref[:, pl.ds(blk * blk_ch, blk_ch)], bm_ref[blk])

    def scan(blk):
        buf = bu_ref.at[blk % 2]
        for q in range(n_state // lane_w):
            re = pl.ds(q * lane_w, lane_w)
            im = pl.ds(n_state + q * lane_w, lane_w)
            a_r = jnp.broadcast_to(ar_ref[blk, :, re], (nb_batch, lane_w))
            a_i = jnp.broadcast_to(ai_ref[blk, :, re], (nb_batch, lane_w))
            h_r, h_i = h_ref[blk, :, re], h_ref[blk, :, im]
            for t in range(tc):
                r = pl.ds(t * nb_batch, nb_batch)
                h_r, h_i = (a_r * h_r - a_i * h_i + buf[r, re],
                            a_r * h_i + a_i * h_r + buf[r, im])
                buf[r, re] = h_r
                buf[r, im] = h_i
            h_ref[blk, :, re] = h_r
            h_ref[blk, :, im] = h_i

    def readout(blk):
        ytb_ref[:, pl.ds(blk * blk_ch, blk_ch)] = _dot(bu_ref[blk % 2].astype(BF16), cm_ref[blk])

    drive(0)
    for blk in range(n_blocks):
        if blk + 1 < n_blocks:
            drive(blk + 1)
        scan(blk)
        readout(blk)

    y = ytb_ref[...] + d_ref[...] * xtb_ref[...]
    z = jax.nn.gelu(y)
    z = z * jax.nn.sigmoid(_dot(z.astype(BF16), gluw_ref[...]) + glub_ref[...])
    zn = _rms(z, gb_ref[...]).astype(BF16)
    out_ref[...] = _dot(permt_ref[...], zn).astype(BF16).reshape(nb_batch, tc, d_ssm)


def _s5(xs4, permt, bm, cm, a_r, a_i, dskip, gluw, glub, gb, *, tc):
    n_tiles, n_slabs, rows, _ = xs4.shape
    d_ssm = n_slabs * LANES
    nb_batch = rows // tc
    seq = n_tiles * tc
    n_blocks, _, two_n = bm.shape
    kern = functools.partial(_s5_kernel, lane_w=256)
    return pl.pallas_call(
        kern,
        grid=(seq // tc,),
        in_specs=[
            pl.BlockSpec((None, n_slabs, rows, LANES), lambda i: (i, 0, 0, 0)),
            _resident(permt.shape),
            _resident(bm.shape),
            _resident(cm.shape),
            _resident(a_r.shape),
            _resident(a_i.shape),
            _resident((1, d_ssm)),
            _resident(gluw.shape),
            _resident((1, d_ssm)),
            _resident((1, d_ssm)),
        ],
        out_specs=pl.BlockSpec((nb_batch, tc, d_ssm), lambda i: (0, i, 0)),
        out_shape=jax.ShapeDtypeStruct((nb_batch, seq, d_ssm), BF16),
        scratch_shapes=[
            pltpu.VMEM((n_blocks, nb_batch, two_n), F32),
            pltpu.VMEM((2, rows, two_n), F32),
            pltpu.VMEM((rows, d_ssm), F32),
            pltpu.VMEM((rows, d_ssm), BF16),
            pltpu.VMEM((rows, d_ssm), F32),
        ],
        compiler_params=_params(("arbitrary",)),
        name="s5",
    )(xs4, permt, bm, cm, a_r, a_i, dskip, gluw, glub, gb)


def _mix_out_kernel(x_ref, ya_ref, yb_ref, wa_ref, wb_ref, g_ref, o_ref, hn_ref):
    x1 = x_ref[...] + _dot(ya_ref[...], wa_ref[...]) + _dot(yb_ref[...], wb_ref[...])
    o_ref[...] = x1
    hn_ref[...] = _rms(x1, g_ref[...]).astype(BF16)


def _mix_out(x2, ya, yb, wa, wb, g, *, tm):
    m, d = x2.shape
    return pl.pallas_call(
        _mix_out_kernel,
        grid=(m // tm,),
        in_specs=[
            pl.BlockSpec((tm, d), lambda i: (i, 0)),
            pl.BlockSpec((tm, ya.shape[1]), lambda i: (i, 0)),
            pl.BlockSpec((tm, yb.shape[1]), lambda i: (i, 0)),
            _resident(wa.shape),
            _resident(wb.shape),
            _resident((1, d)),
        ],
        out_specs=[
            pl.BlockSpec((tm, d), lambda i: (i, 0)),
            pl.BlockSpec((tm, d), lambda i: (i, 0)),
        ],
        out_shape=[
            jax.ShapeDtypeStruct((m, d), F32),
            jax.ShapeDtypeStruct((m, d), BF16),
        ],
        compiler_params=_params(("parallel",)),
        name="mix_out",
    )(x2, ya, yb, wa, wb, g)


def _ffn_kernel(hn_ref, wg_ref, wu_ref, cwg_ref, cwu_ref, cbg_ref, cbu_ref, wd_ref, o_ref,
                carry_ref, a_ref, *, tiles_per_seq, rsub):
    i = pl.program_id(0)
    f = pl.program_id(1)
    tm = hn_ref.shape[0]
    tf = wd_ref.shape[0]
    n_sub = tm // rsub

    @pl.when(f == 0)
    def _():
        o_ref[...] = jnp.zeros_like(o_ref)

    @pl.when(i % tiles_per_seq == 0)
    def _():
        carry_ref[f] = jnp.zeros(carry_ref.shape[1:], F32)

    a_ref[0, pl.ds(0, CARRY), :] = carry_ref[f]

    def project(r):
        h = hn_ref[pl.ds(r * rsub, rsub), :]
        a_ref[r % 2, pl.ds(CARRY, rsub), pl.ds(0, tf)] = _dot(h, wg_ref[...])
        a_ref[r % 2, pl.ds(CARRY, rsub), pl.ds(tf, tf)] = _dot(h, wu_ref[...])

    def conv(slot, cols, cw_ref, cb_ref):
        return (cw_ref[0:1, :] * a_ref[slot, pl.ds(CARRY - 2, rsub), cols]
                + cw_ref[1:2, :] * a_ref[slot, pl.ds(CARRY - 1, rsub), cols]
                + cw_ref[2:3, :] * a_ref[slot, pl.ds(CARRY, rsub), cols] + cb_ref[...])

    project(0)
    for r in range(n_sub):
        slot = r % 2
        if r + 1 < n_sub:
            project(r + 1)
        gate = conv(slot, pl.ds(0, tf), cwg_ref, cbg_ref)
        up = conv(slot, pl.ds(tf, tf), cwu_ref, cbu_ref)
        tail = a_ref[slot, pl.ds(rsub, CARRY), :]
        if r + 1 < n_sub:
            a_ref[1 - slot, pl.ds(0, CARRY), :] = tail
        else:
            carry_ref[f] = tail
        act = (jax.nn.silu(gate) * up).astype(BF16)
        o_ref[pl.ds(r * rsub, rsub), :] += _dot(act, wd_ref[...])


def _ffn(hn, w_up, conv_w, conv_b, w_down, *, tm, tf, rsub, seq):
    m, d = hn.shape
    d_ff = w_down.shape[0]
    nf = d_ff // tf
    kern = functools.partial(_ffn_kernel, tiles_per_seq=seq // tm, rsub=rsub)
    return pl.pallas_call(
        kern,
        grid=(m // tm, nf),
        in_specs=[
            pl.BlockSpec((tm, d), lambda i, f: (i, 0)),
            pl.BlockSpec((d, tf), lambda i, f: (0, f)),
            pl.BlockSpec((d, tf), lambda i, f: (0, nf + f)),
            pl.BlockSpec((CONV_W, tf), lambda i, f: (0, f)),
            pl.BlockSpec((CONV_W, tf), lambda i, f: (0, nf + f)),
            pl.BlockSpec((1, tf), lambda i, f: (0, f)),
            pl.BlockSpec((1, tf), lambda i, f: (0, nf + f)),
            pl.BlockSpec((tf, d), lambda i, f: (f, 0)),
        ],
        out_specs=pl.BlockSpec((tm, d), lambda i, f: (i, 0)),
        out_shape=jax.ShapeDtypeStruct((m, d), F32),
        scratch_shapes=[
            pltpu.VMEM((nf, CARRY, 2 * tf), F32),
            pltpu.VMEM((2, CARRY + rsub, 2 * tf), F32),
        ],
        compiler_params=_params(("arbitrary", "arbitrary")),
        name="ffn",
    )(hn, w_up, w_up, conv_w, conv_w, conv_b, conv_b, w_down)


def _ple_kernel(x_ref, dl_ref, p_ref, g_ref, wg_ref, wp_ref, gf_ref, o_ref, *, final):
    x = x_ref[...] + dl_ref[...]
    gate = jax.nn.sigmoid(_dot(_rms(x, g_ref[...]).astype(BF16), wg_ref[...]))
    y = x + gate * _dot(p_ref[...].astype(BF16), wp_ref[...])
    if final:
        y = _rms(y, gf_ref[...])
    o_ref[...] = y


def _ple(x2, delta, p2, g, wg, wp, gf, *, tm, final):
    m, d = x2.shape
    kern = functools.partial(_ple_kernel, final=final)
    return pl.pallas_call(
        kern,
        grid=(m // tm,),
        in_specs=[
            pl.BlockSpec((tm, d), lambda i: (i, 0)),
            pl.BlockSpec((tm, d), lambda i: (i, 0)),
            pl.BlockSpec((tm, p2.shape[1]), lambda i: (i, 0)),
            _resident((1, d)),
            _resident(wg.shape),
            _resident(wp.shape),
            _resident((1, d)),
        ],
        out_specs=pl.BlockSpec((tm, d), lambda i: (i, 0)),
        out_shape=jax.ShapeDtypeStruct((m, d), F32),
        compiler_params=_params(("parallel",)),
        name="ple_final" if final else "ple",
    )(x2, delta, p2, g, wg, wp, gf)


def _s5_params(lam_re, lam_im, log_dt, b_re, b_im, c_re, c_im):
    n_groups, n_state = lam_re.shape
    dt = jnp.exp(log_dt)[:, None]
    mag = jnp.exp(lam_re * dt)
    ab_r = mag * jnp.cos(lam_im * dt)
    ab_i = mag * jnp.sin(lam_im * dt)
    den = lam_re * lam_re + lam_im * lam_im
    nr = ab_r - 1.0
    q_r = (nr * lam_re + ab_i * lam_im) / den
    q_i = (ab_i * lam_re - nr * lam_im) / den
    bb_r = q_r[..., None] * b_re - q_i[..., None] * b_im
    bb_i = q_r[..., None] * b_im + q_i[..., None] * b_re
    gb = SSM_BLOCK_GROUPS
    n_blocks = n_groups // gb
    eye = jnp.eye(gb, dtype=F32)

    def pack_b(bb):
        bb = bb.reshape(n_blocks, gb, n_state, SSM_GROUP)
        out = jnp.einsum('kgnc,gh->kgchn', bb, eye)
        return out.reshape(n_blocks, gb * SSM_GROUP, gb * n_state)

    def pack_c(cc):
        cc = cc.reshape(n_blocks, gb, SSM_GROUP, n_state)
        out = jnp.einsum('kgcn,gh->kgnhc', cc, eye)
        return out.reshape(n_blocks, gb * n_state, gb * SSM_GROUP)

    bm = jnp.concatenate([pack_b(bb_r), pack_b(bb_i)], axis=-1).astype(BF16)
    cm = jnp.concatenate([pack_c(c_re), pack_c(-c_im)], axis=1).astype(BF16)
    a_r = ab_r.reshape(n_blocks, 1, gb * n_state)
    a_i = ab_i.reshape(n_blocks, 1, gb * n_state)
    return bm, cm, a_r, a_i


def _row_perm(nb_batch, tc):
    dst = jnp.arange(nb_batch * tc)
    src = (dst % nb_batch) * tc + dst // nb_batch
    return (src[:, None] == jnp.arange(nb_batch * tc)[None, :]).astype(BF16)


def _tiles(m, seq):
    tm = min(512, seq)
    tc = min(32, seq)
    tm_ffn = min(1024, seq)
    rsub = min(256, tm_ffn)
    return tm, tc, tm_ffn, rsub


def kernel(x, p, mix_norm, w_in, sgu_norm, sgu_w, sgu_b, s5_lam_re, s5_lam_im, s5_log_dt,
           s5_b_re, s5_b_im, s5_c_re, s5_c_im, s5_d, s5_glu_w, s5_glu_b,
           out_norm_a, out_norm_b, w_out, ffn_norm, ffn_w_up, ffn_conv_w, ffn_conv_b,
           ffn_w_down, ple_norm, ple_w_gate, ple_w_proj, final_norm):
    depth = w_in.shape[0]
    nb_batch, seq, d = x.shape
    m = nb_batch * seq
    d_sgu = sgu_norm.shape[-1]
    d_ssm = s5_d.shape[-1]
    d_ff = ffn_w_down.shape[1]
    n_heads = d_sgu // SGU_HEAD
    assert seq % SGU_CHUNK == 0 and d_sgu % SGU_HEAD == 0
    assert s5_lam_re.shape[1] % SSM_BLOCK_GROUPS == 0
    tm, tc, tm_ffn, rsub = _tiles(m, seq)
    tf = 512 if d_ff % 512 == 0 else d_ff
    assert seq % tm == 0 and tm % SGU_CHUNK == 0 and seq % tc == 0 and tc % 8 == 0
    assert seq % tm_ffn == 0 and tm_ffn % rsub == 0 and rsub % CARRY == 0

    permt = _row_perm(nb_batch, tc).T
    tril = jnp.tril(jnp.ones((SGU_CHUNK, SGU_CHUNK), F32))
    row = lambda v: v.reshape(1, -1)

    x2 = x.reshape(m, d)
    for i in range(depth):
        ws = (sgu_w[i] * tril[None]).astype(BF16)
        bs_full = jnp.repeat(jnp.transpose(sgu_b[i]), SGU_HEAD, axis=1)
        ya, xs = _mix_in(x2, row(mix_norm[i]), w_in[i].astype(BF16), row(sgu_norm[i]), ws, bs_full,
                         row(out_norm_a[i]), tm=tm, tc=tc, seq=seq)
        bm, cm, a_r, a_i = _s5_params(s5_lam_re[i], s5_lam_im[i], s5_log_dt[i], s5_b_re[i], s5_b_im[i],
                                      s5_c_re[i], s5_c_im[i])
        yb = _s5(xs, permt, bm, cm, a_r, a_i, row(s5_d[i]),
                 s5_glu_w[i].astype(BF16), row(s5_glu_b[i]), row(out_norm_b[i]), tc=tc)
        wo = w_out[i].astype(BF16)
        x2, hn = _mix_out(x2, ya, yb.reshape(m, d_ssm), wo[:d_sgu], wo[d_sgu:], row(ffn_norm[i]), tm=tm)
        delta = _ffn(hn, ffn_w_up[i].astype(BF16), ffn_conv_w[i], row(ffn_conv_b[i]),
                     ffn_w_down[i].astype(BF16), tm=tm_ffn, tf=tf, rsub=rsub, seq=seq)
        x2 = _ple(x2, delta, p[i].reshape(m, -1), row(ple_norm[i]), ple_w_gate[i].astype(BF16),
                  ple_w_proj[i].astype(BF16), row(final_norm), tm=tm, final=(i == depth - 1))
    return x2.reshape(nb_batch, seq, d)
```

```python
import functools
import math

import jax
import jax.numpy as jnp
from jax import lax
from jax.experimental import pallas as pl
from jax.experimental.pallas import tpu as pltpu

EPS = 1e-6
SGU_CHUNK = 128
SGU_HEAD = 128
SSM_GROUP = 16
CONV_W = 3
LANES = 128
SSM_BLOCK_GROUPS = LANES // SSM_GROUP
GATHER_PAD = 8
CARRY = 8
V7X_VMEM_LIMIT_BYTES = 56 * 1024 * 1024

F32 = jnp.float32
BF16 = jnp.bfloat16


def _dot(a, b):
    return jnp.dot(a, b, preferred_element_type=F32)


def _rms(x, g):
    return x * lax.rsqrt(jnp.mean(x * x, axis=-1, keepdims=True) + EPS) * g


def _resident(shape):
    nd = len(shape)
    return pl.BlockSpec(shape, lambda *_: (0,) * nd, pipeline_mode=pl.Buffered(1))


def _params(semantics):
    return pltpu.CompilerParams(dimension_semantics=semantics,
                                vmem_limit_bytes=V7X_VMEM_LIMIT_BYTES)


def _mix_in_kernel(x_ref, gmix_ref, win_ref, gsgu_ref, ws_ref, bs_ref, ga_ref,
                   ya_ref, xs_ref, u_ref, vn_ref, *, d_sgu, n_heads):
    tm = x_ref.shape[0]
    h = _rms(x_ref[...], gmix_ref[...]).astype(BF16)
    u_ref[...] = jax.nn.gelu(_dot(h, win_ref[:, :d_sgu]))
    v = jax.nn.gelu(_dot(h, win_ref[:, d_sgu:2 * d_sgu]))
    xs = _dot(h, win_ref[:, 2 * d_sgu:])
    n_tt, n_slabs, pitch, _ = xs_ref.shape
    tc = pitch - GATHER_PAD
    for j in range(n_slabs):
        xs_ref[:, j, pl.ds(0, tc), :] = xs[:, j * LANES:(j + 1) * LANES].reshape(n_tt, tc, LANES)
        xs_ref[:, j, pl.ds(tc, GATHER_PAD), :] = jnp.zeros((n_tt, GATHER_PAD, LANES), F32)
    vc = v - jnp.mean(v, axis=-1, keepdims=True)
    vn = vc * lax.rsqrt(jnp.mean(vc * vc, axis=-1, keepdims=True) + EPS) * gsgu_ref[...]
    vn_ref[...] = vn.astype(BF16)
    for c in range(tm // SGU_CHUNK):
        rows = pl.ds(c * SGU_CHUNK, SGU_CHUNK)
        for hd in range(n_heads):
            cols = pl.ds(hd * SGU_HEAD, SGU_HEAD)
            sv = _dot(ws_ref[hd], vn_ref[rows, cols]) + bs_ref[:, cols]
            u_ref[rows, cols] = u_ref[rows, cols] * sv
    ya_ref[...] = _rms(u_ref[...], ga_ref[...]).astype(BF16)


def _mix_in(x2, gmix, win, gsgu, ws, bs_full, ga, *, tm, tc, seq):
    m, d = x2.shape
    tiles_per_seq = seq // tm
    d_sgu = gsgu.shape[-1]
    n_slabs = (win.shape[1] - 2 * d_sgu) // LANES
    n_heads = d_sgu // SGU_HEAD
    kern = functools.partial(_mix_in_kernel, d_sgu=d_sgu, n_heads=n_heads)
    return pl.pallas_call(
        kern,
        grid=(m // tm,),
        in_specs=[
            pl.BlockSpec((tm, d), lambda i: (i, 0)),
            _resident((1, d)),
            _resident(win.shape),
            _resident((1, d_sgu)),
            _resident(ws.shape),
            _resident(bs_full.shape),
            _resident((1, d_sgu)),
        ],
        out_specs=[
            pl.BlockSpec((tm, d_sgu), lambda i: (i, 0)),
            pl.BlockSpec((tm // tc, n_slabs, tc + GATHER_PAD, LANES),
                         lambda i: (i % tiles_per_seq, 0, i // tiles_per_seq, 0)),
        ],
        out_shape=[
            jax.ShapeDtypeStruct((m, d_sgu), BF16),
            jax.ShapeDtypeStruct((seq // tc, n_slabs, (m // seq) * (tc + GATHER_PAD), LANES), F32),
        ],
        scratch_shapes=[
            pltpu.VMEM((tm, d_sgu), F32),
            pltpu.VMEM((tm, d_sgu), BF16),
        ],
        compiler_params=_params(("parallel",)),
        name="mix_in",
    )(x2, gmix, win, gsgu, ws, bs_full, ga)


def _s5_kernel(xs_ref, permt_ref, wd_ref, wr_ref, wx_ref, a2r_ref, a2i_ref, d_ref,
               gluw_ref, glub_ref, gb_ref, out_ref,
               h_ref, dr_ref, sp_ref, xtok_ref, xpair_ref, ytok_ref, *, lane_w):
    n_slabs, prow, _ = xs_ref.shape
    nb_batch, tc, d_ssm = out_ref.shape
    pitch = prow // nb_batch
    n_blocks, _, two_n = wd_ref.shape
    n_state = two_n // 2
    n_pairs = tc // 2
    hp = n_pairs // 2
    hrows = hp * nb_batch
    pair_lanes = 2 * LANES

    @pl.when(pl.program_id(0) == 0)
    def _():
        h_ref[...] = jnp.zeros_like(h_ref)

    def tok_rows(half, parity):
        return pl.ds((2 * half + parity) * hrows, hrows)

    for j in range(n_slabs):
        for t in range(tc):
            tau, parity = divmod(t, 2)
            half, tau_h = divmod(tau, hp)
            tile = xs_ref[j, pl.ds(t, nb_batch, stride=pitch), :]
            xtok_ref[pl.ds((2 * half + parity) * hrows + tau_h * nb_batch, nb_batch),
                     pl.ds(j * LANES, LANES)] = tile
            xpair_ref[pl.ds(tau * nb_batch, nb_batch),
                      pl.ds((2 * j + parity) * LANES, LANES)] = tile.astype(BF16)

    def drive(blk):
        dr_ref[blk % 2] = _dot(xpair_ref[:, pl.ds(blk * pair_lanes, pair_lanes)], wd_ref[blk])

    def scan(blk):
        buf = dr_ref.at[blk % 2]
        sp = sp_ref.at[blk % 2]
        for q in range(n_state // lane_w):
            re = pl.ds(q * lane_w, lane_w)
            im = pl.ds(n_state + q * lane_w, lane_w)
            a_r = jnp.broadcast_to(a2r_ref[blk, :, re], (nb_batch, lane_w))
            a_i = jnp.broadcast_to(a2i_ref[blk, :, re], (nb_batch, lane_w))
            h_r, h_i = h_ref[blk, :, re], h_ref[blk, :, im]
            for tau in range(n_pairs):
                r = pl.ds(tau * nb_batch, nb_batch)
                sp[r, re] = h_r.astype(BF16)
                sp[r, im] = h_i.astype(BF16)
                h_r, h_i = (a_r * h_r - a_i * h_i + buf[r, re],
                            a_r * h_i + a_i * h_r + buf[r, im])
            h_ref[blk, :, re] = h_r
            h_ref[blk, :, im] = h_i

    def readout(blk):
        yy = (_dot(sp_ref[blk % 2], wr_ref[blk])
              + _dot(xpair_ref[:, pl.ds(blk * pair_lanes, pair_lanes)], wx_ref[blk]))
        ch = pl.ds(blk * LANES, LANES)
        for half in range(2):
            for parity in range(2):
                dst = tok_rows(half, parity)
                ytok_ref[dst, ch] = (yy[half * hrows:(half + 1) * hrows, parity * LANES:(parity + 1) * LANES]
                                     + d_ref[:, ch] * xtok_ref[dst, ch])

    drive(0)
    for blk in range(n_blocks):
        if blk + 1 < n_blocks:
            drive(blk + 1)
        scan(blk)
        readout(blk)

    z = jax.nn.gelu(ytok_ref[...])
    z = z * jax.nn.sigmoid(_dot(z.astype(BF16), gluw_ref[...]) + glub_ref[...])
    zn = _rms(z, gb_ref[...]).astype(BF16)
    for half in range(2):
        out_ref[:, pl.ds(half * (tc // 2), tc // 2), :] = (
            _dot(permt_ref[...], zn[2 * half * hrows:2 * (half + 1) * hrows])
            .astype(BF16).reshape(nb_batch, tc // 2, d_ssm))


def _s5(xs4, permt, wd, wr, wx, a2_r, a2_i, dskip, gluw, glub, gb, *, tc, nb_batch):
    n_tiles, n_slabs, prow, _ = xs4.shape
    d_ssm = n_slabs * LANES
    seq = n_tiles * tc
    rows = nb_batch * tc
    n_blocks, _, two_n = wd.shape
    assert n_blocks == n_slabs and prow == nb_batch * (tc + GATHER_PAD)
    kern = functools.partial(_s5_kernel, lane_w=256)
    return pl.pallas_call(
        kern,
        grid=(n_tiles,),
        in_specs=[
            pl.BlockSpec((None, n_slabs, prow, LANES), lambda i: (i, 0, 0, 0)),
            _resident(permt.shape),
            _resident(wd.shape),
            _resident(wr.shape),
            _resident(wx.shape),
            _resident(a2_r.shape),
            _resident(a2_i.shape),
            _resident((1, d_ssm)),
            _resident(gluw.shape),
            _resident((1, d_ssm)),
            _resident((1, d_ssm)),
        ],
        out_specs=pl.BlockSpec((nb_batch, tc, d_ssm), lambda i: (0, i, 0)),
        out_shape=jax.ShapeDtypeStruct((nb_batch, seq, d_ssm), BF16),
        scratch_shapes=[
            pltpu.VMEM((n_blocks, nb_batch, two_n), F32),
            pltpu.VMEM((2, rows // 2, two_n), F32),
            pltpu.VMEM((2, rows // 2, two_n), BF16),
            pltpu.VMEM((rows, d_ssm), F32),
            pltpu.VMEM((rows // 2, 2 * d_ssm), BF16),
            pltpu.VMEM((rows, d_ssm), F32),
        ],
        compiler_params=_params(("arbitrary",)),
        name="s5",
    )(xs4, permt, wd, wr, wx, a2_r, a2_i, dskip, gluw, glub, gb)


def _mix_out_kernel(x_ref, ya_ref, yb_ref, wa_ref, wb_ref, g_ref, o_ref, hn_ref):
    x1 = x_ref[...] + _dot(ya_ref[...], wa_ref[...]) + _dot(yb_ref[...], wb_ref[...])
    o_ref[...] = x1
    hn_ref[...] = _rms(x1, g_ref[...]).astype(BF16)


def _mix_out(x2, ya, yb, wa, wb, g, *, tm):
    m, d = x2.shape
    return pl.pallas_call(
        _mix_out_kernel,
        grid=(m // tm,),
        in_specs=[
            pl.BlockSpec((tm, d), lambda i: (i, 0)),
            pl.BlockSpec((tm, ya.shape[1]), lambda i: (i, 0)),
            pl.BlockSpec((tm, yb.shape[1]), lambda i: (i, 0)),
            _resident(wa.shape),
            _resident(wb.shape),
            _resident((1, d)),
        ],
        out_specs=[
            pl.BlockSpec((tm, d), lambda i: (i, 0)),
            pl.BlockSpec((tm, d), lambda i: (i, 0)),
        ],
        out_shape=[
            jax.ShapeDtypeStruct((m, d), F32),
            jax.ShapeDtypeStruct((m, d), BF16),
        ],
        compiler_params=_params(("parallel",)),
        name="mix_out",
    )(x2, ya, yb, wa, wb, g)


def _ffn_kernel(hn_ref, wg_ref, wu_ref, cwg_ref, cwu_ref, cbg_ref, cbu_ref, wd_ref, o_ref,
                carry_ref, a_ref, *, tiles_per_seq, rsub):
    i = pl.program_id(0)
    f = pl.program_id(1)
    tm = hn_ref.shape[0]
    tf = wd_ref.shape[0]
    n_sub = tm // rsub

    @pl.when(f == 0)
    def _():
        o_ref[...] = jnp.zeros_like(o_ref)

    @pl.when(i % tiles_per_seq == 0)
    def _():
        carry_ref[f] = jnp.zeros(carry_ref.shape[1:], F32)

    a_ref[0, pl.ds(0, CARRY), :] = carry_ref[f]

    def project(r):
        h = hn_ref[pl.ds(r * rsub, rsub), :]
        a_ref[r % 2, pl.ds(CARRY, rsub), pl.ds(0, tf)] = _dot(h, wg_ref[...])
        a_ref[r % 2, pl.ds(CARRY, rsub), pl.ds(tf, tf)] = _dot(h, wu_ref[...])

    def conv(slot, cols, cw_ref, cb_ref):
        return (cw_ref[0:1, :] * a_ref[slot, pl.ds(CARRY - 2, rsub), cols]
                + cw_ref[1:2, :] * a_ref[slot, pl.ds(CARRY - 1, rsub), cols]
                + cw_ref[2:3, :] * a_ref[slot, pl.ds(CARRY, rsub), cols] + cb_ref[...])

    project(0)
    for r in range(n_sub):
        slot = r % 2
        if r + 1 < n_sub:
            project(r + 1)
        gate = conv(slot, pl.ds(0, tf), cwg_ref, cbg_ref)
        up = conv(slot, pl.ds(tf, tf), cwu_ref, cbu_ref)
        tail = a_ref[slot, pl.ds(rsub, CARRY), :]
        if r + 1 < n_sub:
            a_ref[1 - slot, pl.ds(0, CARRY), :] = tail
        else:
            carry_ref[f] = tail
        act = (jax.nn.silu(gate) * up).astype(BF16)
        o_ref[pl.ds(r * rsub, rsub), :] += _dot(act, wd_ref[...])


def _ffn(hn, w_up, conv_w, conv_b, w_down, *, tm, tf, rsub, seq):
    m, d = hn.shape
    d_ff = w_down.shape[0]
    nf = d_ff // tf
    kern = functools.partial(_ffn_kernel, tiles_per_seq=seq // tm, rsub=rsub)
    return pl.pallas_call(
        kern,
        grid=(m // tm, nf),
        in_specs=[
            pl.BlockSpec((tm, d), lambda i, f: (i, 0)),
            pl.BlockSpec((d, tf), lambda i, f: (0, f)),
            pl.BlockSpec((d, tf), lambda i, f: (0, nf + f)),
            pl.BlockSpec((CONV_W, tf), lambda i, f: (0, f)),
            pl.BlockSpec((CONV_W, tf), lambda i, f: (0, nf + f)),
            pl.BlockSpec((1, tf), lambda i, f: (0, f)),
            pl.BlockSpec((1, tf), lambda i, f: (0, nf + f)),
            pl.BlockSpec((tf, d), lambda i, f: (f, 0)),
        ],
        out_specs=pl.BlockSpec((tm, d), lambda i, f: (i, 0)),
        out_shape=jax.ShapeDtypeStruct((m, d), F32),
        scratch_shapes=[
            pltpu.VMEM((nf, CARRY, 2 * tf), F32),
            pltpu.VMEM((2, CARRY + rsub, 2 * tf), F32),
        ],
        compiler_params=_params(("arbitrary", "arbitrary")),
        name="ffn",
    )(hn, w_up, w_up, conv_w, conv_w, conv_b, conv_b, w_down)


def _ple_kernel(x_ref, dl_ref, p_ref, g_ref, wg_ref, wp_ref, gf_ref, o_ref, *, final):
    x = x_ref[...] + dl_ref[...]
    gate = jax.nn.sigmoid(_dot(_rms(x, g_ref[...]).astype(BF16), wg_ref[...]))
    y = x + gate * _dot(p_ref[...].astype(BF16), wp_ref[...])
    if final:
        y = _rms(y, gf_ref[...])
    o_ref[...] = y


def _ple(x2, delta, p2, g, wg, wp, gf, *, tm, final):
    m, d = x2.shape
    kern = functools.partial(_ple_kernel, final=final)
    return pl.pallas_call(
        kern,
        grid=(m // tm,),
        in_specs=[
            pl.BlockSpec((tm, d), lambda i: (i, 0)),
            pl.BlockSpec((tm, d), lambda i: (i, 0)),
            pl.BlockSpec((tm, p2.shape[1]), lambda i: (i, 0)),
            _resident((1, d)),
            _resident(wg.shape),
            _resident(wp.shape),
            _resident((1, d)),
        ],
        out_specs=pl.BlockSpec((tm, d), lambda i: (i, 0)),
        out_shape=jax.ShapeDtypeStruct((m, d), F32),
        compiler_params=_params(("parallel",)),
        name="ple_final" if final else "ple",
    )(x2, delta, p2, g, wg, wp, gf)


def _s5_params(lam_re, lam_im, log_dt, b_re, b_im, c_re, c_im):
    n_groups, n_state = lam_re.shape
    hi = lax.Precision.HIGHEST
    dt = jnp.exp(log_dt)[:, None]
    mag = jnp.exp(lam_re * dt)
    ab_r = mag * jnp.cos(lam_im * dt)
    ab_i = mag * jnp.sin(lam_im * dt)
    den = lam_re * lam_re + lam_im * lam_im
    nr = ab_r - 1.0
    q_r = (nr * lam_re + ab_i * lam_im) / den
    q_i = (ab_i * lam_re - nr * lam_im) / den
    bb_r = q_r[..., None] * b_re - q_i[..., None] * b_im
    bb_i = q_r[..., None] * b_im + q_i[..., None] * b_re
    abb_r = ab_r[..., None] * bb_r - ab_i[..., None] * bb_i
    abb_i = ab_r[..., None] * bb_i + ab_i[..., None] * bb_r
    a2_r = ab_r * ab_r - ab_i * ab_i
    a2_i = 2.0 * ab_r * ab_i
    ca_r = c_re * ab_r[:, None, :] - c_im * ab_i[:, None, :]
    ca_i = c_re * ab_i[:, None, :] + c_im * ab_r[:, None, :]
    ca2_r = c_re * a2_r[:, None, :] - c_im * a2_i[:, None, :]
    ca2_i = c_re * a2_i[:, None, :] + c_im * a2_r[:, None, :]
    k0 = (jnp.einsum('gon,gni->gio', c_re, bb_r, precision=hi)
          - jnp.einsum('gon,gni->gio', c_im, bb_i, precision=hi))
    k1 = (jnp.einsum('gon,gni->gio', c_re, abb_r, precision=hi)
          - jnp.einsum('gon,gni->gio', c_im, abb_i, precision=hi))

    gb = SSM_BLOCK_GROUPS
    n_blocks = n_groups // gb
    eye = jnp.eye(gb, dtype=F32)

    def diag(w):
        p, q = w.shape[1:]
        w = w.reshape(n_blocks, gb, p, q)
        return (w[:, :, :, None, :] * eye[None, :, None, :, None]).reshape(n_blocks, gb * p, gb * q)

    t = lambda w: jnp.swapaxes(w, 1, 2)
    wd = jnp.concatenate([
        jnp.concatenate([diag(t(abb_r)), diag(t(abb_i))], axis=2),
        jnp.concatenate([diag(t(bb_r)), diag(t(bb_i))], axis=2),
    ], axis=1).astype(BF16)
    wr = jnp.concatenate([
        jnp.concatenate([diag(t(ca_r)), diag(t(-ca_i))], axis=1),
        jnp.concatenate([diag(t(ca2_r)), diag(t(-ca2_i))], axis=1),
    ], axis=2).astype(BF16)
    dk0, dk1 = diag(k0), diag(k1)
    wx = jnp.concatenate([
        jnp.concatenate([dk0, dk1], axis=2),
        jnp.concatenate([jnp.zeros_like(dk0), dk0], axis=2),
    ], axis=1).astype(BF16)
    return (wd, wr, wx, a2_r.reshape(n_blocks, 1, gb * n_state), a2_i.reshape(n_blocks, 1, gb * n_state))


def _token_perm(nb_batch, steps):
    dst = jnp.arange(nb_batch * steps)
    b, t = dst // steps, dst % steps
    src = (t % 2) * (steps // 2) * nb_batch + (t // 2) * nb_batch + b
    return (src[:, None] == jnp.arange(nb_batch * steps)[None, :]).astype(BF16)


def _tiles(m, seq):
    tm = min(512, seq)
    tc = min(64, seq)
    tm_ffn = min(1024, seq)
    rsub = min(256, tm_ffn)
    return tm, tc, tm_ffn, rsub


def kernel(x, p, mix_norm, w_in, sgu_norm, sgu_w, sgu_b, s5_lam_re, s5_lam_im, s5_log_dt,
           s5_b_re, s5_b_im, s5_c_re, s5_c_im, s5_d, s5_glu_w, s5_glu_b,
           out_norm_a, out_norm_b, w_out, ffn_norm, ffn_w_up, ffn_conv_w, ffn_conv_b,
           ffn_w_down, ple_norm, ple_w_gate, ple_w_proj, final_norm):
    depth = w_in.shape[0]
    nb_batch, seq, d = x.shape
    m = nb_batch * seq
    d_sgu = sgu_norm.shape[-1]
    d_ssm = s5_d.shape[-1]
    d_ff = ffn_w_down.shape[1]
    n_heads = d_sgu // SGU_HEAD
    assert seq % SGU_CHUNK == 0 and d_sgu % SGU_HEAD == 0
    assert s5_lam_re.shape[1] % SSM_BLOCK_GROUPS == 0
    tm, tc, tm_ffn, rsub = _tiles(m, seq)
    tf = 512 if d_ff % 512 == 0 else d_ff
    assert seq % tm == 0 and tm % SGU_CHUNK == 0 and seq % tc == 0 and tc % 32 == 0
    assert seq % tm_ffn == 0 and tm_ffn % rsub == 0 and rsub % CARRY == 0

    permt = _token_perm(nb_batch, tc // 2)
    tril = jnp.tril(jnp.ones((SGU_CHUNK, SGU_CHUNK), F32))
    row = lambda v: v.reshape(1, -1)

    x2 = x.reshape(m, d)
    for i in range(depth):
        ws = (sgu_w[i] * tril[None]).astype(BF16)
        bs_full = jnp.repeat(jnp.transpose(sgu_b[i]), SGU_HEAD, axis=1)
        ya, xs = _mix_in(x2, row(mix_norm[i]), w_in[i].astype(BF16), row(sgu_norm[i]), ws, bs_full,
                         row(out_norm_a[i]), tm=tm, tc=tc, seq=seq)
        s5w = _s5_params(s5_lam_re[i], s5_lam_im[i], s5_log_dt[i], s5_b_re[i], s5_b_im[i],
                         s5_c_re[i], s5_c_im[i])
        yb = _s5(xs, permt, *s5w, row(s5_d[i]), s5_glu_w[i].astype(BF16), row(s5_glu_b[i]),
                 row(out_norm_b[i]), tc=tc, nb_batch=nb_batch)
        wo = w_out[i].astype(BF16)
        x2, hn = _mix_out(x2, ya, yb.reshape(m, d_ssm), wo[:d_sgu], wo[d_sgu:], row(ffn_norm[i]), tm=tm)
        delta = _ffn(hn, ffn_w_up[i].astype(BF16), ffn_conv_w[i], row(ffn_conv_b[i]),
                     ffn_w_down[i].astype(BF16), tm=tm_ffn, tf=tf, rsub=rsub, seq=seq)
        x2 = _ple(x2, delta, p[i].reshape(m, -1), row(ple_norm[i]), ple_w_gate[i].astype(BF16),
                  ple_w_proj[i].astype(BF16), row(final_norm), tm=tm, final=(i == depth - 1))
    return x2.reshape(nb_batch, seq, d)
```

```python
import functools
import math

import jax
import jax.numpy as jnp
from jax import lax
from jax.experimental import pallas as pl
from jax.experimental.pallas import tpu as pltpu

EPS = 1e-6
SGU_CHUNK = 128
SGU_HEAD = 128
SSM_GROUP = 16
CONV_W = 3
LANES = 128
SSM_BLOCK_GROUPS = LANES // SSM_GROUP
GATHER_PAD = 8
CARRY = 8
V7X_VMEM_LIMIT_BYTES = 56 * 1024 * 1024

F32 = jnp.float32
BF16 = jnp.bfloat16


def _dot(a, b):
    return jnp.dot(a, b, preferred_element_type=F32)


def _rms(x, g):
    return x * lax.rsqrt(jnp.mean(x * x, axis=-1, keepdims=True) + EPS) * g


def _resident(shape):
    nd = len(shape)
    return pl.BlockSpec(shape, lambda *_: (0,) * nd, pipeline_mode=pl.Buffered(1))


def _layer(shape, layer, block=0):
    nd = len(shape)
    return pl.BlockSpec((None,) + tuple(shape), lambda *_: (layer, block) + (0,) * (nd - 1),
                        pipeline_mode=pl.Buffered(1))


def _params(semantics):
    return pltpu.CompilerParams(dimension_semantics=semantics,
                                vmem_limit_bytes=V7X_VMEM_LIMIT_BYTES)


def _mix_in_kernel(x_ref, gmix_ref, win_ref, gsgu_ref, ws_ref, bs_ref, ga_ref,
                   ya_ref, xs_ref, u_ref, vn_ref, *, d_sgu, n_heads):
    tm = x_ref.shape[0]
    h = _rms(x_ref[...], gmix_ref[...]).astype(BF16)
    u_ref[...] = jax.nn.gelu(_dot(h, win_ref[:, :d_sgu]))
    v = jax.nn.gelu(_dot(h, win_ref[:, d_sgu:2 * d_sgu]))
    xs = _dot(h, win_ref[:, 2 * d_sgu:])
    n_tt, n_slabs, pitch, _ = xs_ref.shape
    tc = pitch - GATHER_PAD
    for j in range(n_slabs):
        xs_ref[:, j, pl.ds(0, tc), :] = xs[:, j * LANES:(j + 1) * LANES].reshape(n_tt, tc, LANES)
        xs_ref[:, j, pl.ds(tc, GATHER_PAD), :] = jnp.zeros((n_tt, GATHER_PAD, LANES), F32)
    vc = v - jnp.mean(v, axis=-1, keepdims=True)
    vn = vc * lax.rsqrt(jnp.mean(vc * vc, axis=-1, keepdims=True) + EPS) * gsgu_ref[...]
    vn_ref[...] = vn.astype(BF16)
    for c in range(tm // SGU_CHUNK):
        rows = pl.ds(c * SGU_CHUNK, SGU_CHUNK)
        for hd in range(n_heads):
            cols = pl.ds(hd * SGU_HEAD, SGU_HEAD)
            sv = _dot(ws_ref[hd], vn_ref[rows, cols]) + bs_ref[:, cols]
            u_ref[rows, cols] = u_ref[rows, cols] * sv
    ya_ref[...] = _rms(u_ref[...], ga_ref[...]).astype(BF16)


def _mix_in(x2, gmix, win, gsgu, ws, bs_full, ga, *, layer, tm, tc, seq):
    m, d = x2.shape
    tiles_per_seq = seq // tm
    d_sgu = gsgu.shape[-1]
    n_slabs = (win.shape[2] - 2 * d_sgu) // LANES
    n_heads = d_sgu // SGU_HEAD
    kern = functools.partial(_mix_in_kernel, d_sgu=d_sgu, n_heads=n_heads)
    return pl.pallas_call(
        kern,
        grid=(m // tm,),
        in_specs=[
            pl.BlockSpec((tm, d), lambda i: (i, 0)),
            _resident((1, d)),
            _layer(win.shape[1:], layer),
            _resident((1, d_sgu)),
            _resident(ws.shape),
            _resident(bs_full.shape),
            _resident((1, d_sgu)),
        ],
        out_specs=[
            pl.BlockSpec((tm, d_sgu), lambda i: (i, 0)),
            pl.BlockSpec((tm // tc, n_slabs, tc + GATHER_PAD, LANES),
                         lambda i: (i % tiles_per_seq, 0, i // tiles_per_seq, 0)),
        ],
        out_shape=[
            jax.ShapeDtypeStruct((m, d_sgu), BF16),
            jax.ShapeDtypeStruct((seq // tc, n_slabs, (m // seq) * (tc + GATHER_PAD), LANES), F32),
        ],
        scratch_shapes=[
            pltpu.VMEM((tm, d_sgu), F32),
            pltpu.VMEM((tm, d_sgu), BF16),
        ],
        compiler_params=_params(("parallel",)),
        name="mix_in",
    )(x2, gmix, win, gsgu, ws, bs_full, ga)


def _s5_kernel(xs_ref, permt_ref, wd_ref, wr_ref, wx_ref, a2r_ref, a2i_ref, d_ref,
               gluw_ref, glub_ref, gb_ref, out_ref,
               h_ref, dr_ref, sp_ref, xtok_ref, xpair_ref, ytok_ref, *, lane_w):
    n_slabs, prow, _ = xs_ref.shape
    nb_batch, tc, d_ssm = out_ref.shape
    pitch = prow // nb_batch
    n_blocks, _, two_n = wd_ref.shape
    n_state = two_n // 2
    n_pairs = tc // 2
    hp = n_pairs // 2
    hrows = hp * nb_batch
    pair_lanes = 2 * LANES

    @pl.when(pl.program_id(0) == 0)
    def _():
        h_ref[...] = jnp.zeros_like(h_ref)

    def tok_rows(half, parity):
        return pl.ds((2 * half + parity) * hrows, hrows)

    for j in range(n_slabs):
        for t in range(tc):
            tau, parity = divmod(t, 2)
            half, tau_h = divmod(tau, hp)
            tile = xs_ref[j, pl.ds(t, nb_batch, stride=pitch), :]
            xtok_ref[pl.ds((2 * half + parity) * hrows + tau_h * nb_batch, nb_batch),
                     pl.ds(j * LANES, LANES)] = tile
            xpair_ref[pl.ds(tau * nb_batch, nb_batch),
                      pl.ds((2 * j + parity) * LANES, LANES)] = tile.astype(BF16)

    def drive(blk):
        dr_ref[blk % 2] = _dot(xpair_ref[:, pl.ds(blk * pair_lanes, pair_lanes)], wd_ref[blk])

    def scan(blk):
        buf = dr_ref.at[blk % 2]
        sp = sp_ref.at[blk % 2]
        for q in range(n_state // lane_w):
            re = pl.ds(q * lane_w, lane_w)
            im = pl.ds(n_state + q * lane_w, lane_w)
            a_r = jnp.broadcast_to(a2r_ref[blk, :, re], (nb_batch, lane_w))
            a_i = jnp.broadcast_to(a2i_ref[blk, :, re], (nb_batch, lane_w))
            h_r, h_i = h_ref[blk, :, re], h_ref[blk, :, im]
            for tau in range(n_pairs):
                r = pl.ds(tau * nb_batch, nb_batch)
                sp[r, re] = h_r.astype(BF16)
                sp[r, im] = h_i.astype(BF16)
                h_r, h_i = (a_r * h_r - a_i * h_i + buf[r, re],
                            a_r * h_i + a_i * h_r + buf[r, im])
            h_ref[blk, :, re] = h_r
            h_ref[blk, :, im] = h_i

    def readout(blk):
        yy = (_dot(sp_ref[blk % 2], wr_ref[blk])
              + _dot(xpair_ref[:, pl.ds(blk * pair_lanes, pair_lanes)], wx_ref[blk]))
        ch = pl.ds(blk * LANES, LANES)
        for half in range(2):
            for parity in range(2):
                dst = tok_rows(half, parity)
                ytok_ref[dst, ch] = (yy[half * hrows:(half + 1) * hrows, parity * LANES:(parity + 1) * LANES]
                                     + d_ref[:, ch] * xtok_ref[dst, ch])

    drive(0)
    for blk in range(n_blocks):
        if blk + 1 < n_blocks:
            drive(blk + 1)
        scan(blk)
        readout(blk)

    z = jax.nn.gelu(ytok_ref[...])
    z = z * jax.nn.sigmoid(_dot(z.astype(BF16), gluw_ref[...]) + glub_ref[...])
    zn = _rms(z, gb_ref[...]).astype(BF16)
    for half in range(2):
        out_ref[:, pl.ds(half * (tc // 2), tc // 2), :] = (
            _dot(permt_ref[...], zn[2 * half * hrows:2 * (half + 1) * hrows])
            .astype(BF16).reshape(nb_batch, tc // 2, d_ssm))


def _s5(xs4, permt, wd, wr, wx, a2_r, a2_i, dskip, gluw, glub, gb, *, layer, tc, nb_batch):
    n_tiles, n_slabs, prow, _ = xs4.shape
    d_ssm = n_slabs * LANES
    seq = n_tiles * tc
    rows = nb_batch * tc
    n_blocks, _, two_n = wd.shape
    assert n_blocks == n_slabs and prow == nb_batch * (tc + GATHER_PAD)
    kern = functools.partial(_s5_kernel, lane_w=256)
    return pl.pallas_call(
        kern,
        grid=(n_tiles,),
        in_specs=[
            pl.BlockSpec((None, n_slabs, prow, LANES), lambda i: (i, 0, 0, 0)),
            _resident(permt.shape),
            _resident(wd.shape),
            _resident(wr.shape),
            _resident(wx.shape),
            _resident(a2_r.shape),
            _resident(a2_i.shape),
            _resident((1, d_ssm)),
            _layer(gluw.shape[1:], layer),
            _resident((1, d_ssm)),
            _resident((1, d_ssm)),
        ],
        out_specs=pl.BlockSpec((nb_batch, tc, d_ssm), lambda i: (0, i, 0)),
        out_shape=jax.ShapeDtypeStruct((nb_batch, seq, d_ssm), BF16),
        scratch_shapes=[
            pltpu.VMEM((n_blocks, nb_batch, two_n), F32),
            pltpu.VMEM((2, rows // 2, two_n), F32),
            pltpu.VMEM((2, rows // 2, two_n), BF16),
            pltpu.VMEM((rows, d_ssm), F32),
            pltpu.VMEM((rows // 2, 2 * d_ssm), BF16),
            pltpu.VMEM((rows, d_ssm), F32),
        ],
        compiler_params=_params(("arbitrary",)),
        name="s5",
    )(xs4, permt, wd, wr, wx, a2_r, a2_i, dskip, gluw, glub, gb)


def _mix_out_kernel(x_ref, ya_ref, yb_ref, wa_ref, wb_ref, g_ref, o_ref, hn_ref):
    x1 = x_ref[...] + _dot(ya_ref[...], wa_ref[...]) + _dot(yb_ref[...], wb_ref[...])
    o_ref[...] = x1
    hn_ref[...] = _rms(x1, g_ref[...]).astype(BF16)


def _mix_out(x2, ya, yb, wo, g, *, layer, tm):
    m, d = x2.shape
    return pl.pallas_call(
        _mix_out_kernel,
        grid=(m // tm,),
        in_specs=[
            pl.BlockSpec((tm, d), lambda i: (i, 0)),
            pl.BlockSpec((tm, ya.shape[1]), lambda i: (i, 0)),
            pl.BlockSpec((tm, yb.shape[1]), lambda i: (i, 0)),
            _layer((ya.shape[1], d), layer, 0),
            _layer((yb.shape[1], d), layer, 1),
            _resident((1, d)),
        ],
        out_specs=[
            pl.BlockSpec((tm, d), lambda i: (i, 0)),
            pl.BlockSpec((tm, d), lambda i: (i, 0)),
        ],
        out_shape=[
            jax.ShapeDtypeStruct((m, d), F32),
            jax.ShapeDtypeStruct((m, d), BF16),
        ],
        compiler_params=_params(("parallel",)),
        name="mix_out",
    )(x2, ya, yb, wo, wo, g)


def _ffn_kernel(hn_ref, wg_ref, wu_ref, cwg_ref, cwu_ref, cbg_ref, cbu_ref, wd_ref, o_ref,
                carry_ref, a_ref, *, tiles_per_seq, rsub):
    i = pl.program_id(0)
    f = pl.program_id(1)
    tm = hn_ref.shape[0]
    tf = wd_ref.shape[0]
    n_sub = tm // rsub

    @pl.when(f == 0)
    def _():
        o_ref[...] = jnp.zeros_like(o_ref)

    @pl.when(i % tiles_per_seq == 0)
    def _():
        carry_ref[f] = jnp.zeros(carry_ref.shape[1:], F32)

    a_ref[0, pl.ds(0, CARRY), :] = carry_ref[f]

    def project(r):
        h = hn_ref[pl.ds(r * rsub, rsub), :]
        a_ref[r % 2, pl.ds(CARRY, rsub), pl.ds(0, tf)] = _dot(h, wg_ref[...])
        a_ref[r % 2, pl.ds(CARRY, rsub), pl.ds(tf, tf)] = _dot(h, wu_ref[...])

    def conv(slot, cols, cw_ref, cb_ref):
        return (cw_ref[0:1, :] * a_ref[slot, pl.ds(CARRY - 2, rsub), cols]
                + cw_ref[1:2, :] * a_ref[slot, pl.ds(CARRY - 1, rsub), cols]
                + cw_ref[2:3, :] * a_ref[slot, pl.ds(CARRY, rsub), cols] + cb_ref[...])

    project(0)
    for r in range(n_sub):
        slot = r % 2
        if r + 1 < n_sub:
            project(r + 1)
        gate = conv(slot, pl.ds(0, tf), cwg_ref, cbg_ref)
        up = conv(slot, pl.ds(tf, tf), cwu_ref, cbu_ref)
        tail = a_ref[slot, pl.ds(rsub, CARRY), :]
        if r + 1 < n_sub:
            a_ref[1 - slot, pl.ds(0, CARRY), :] = tail
        else:
            carry_ref[f] = tail
        act = (jax.nn.silu(gate) * up).astype(BF16)
        o_ref[pl.ds(r * rsub, rsub), :] += _dot(act, wd_ref[...])


def _ffn(hn, w_up, conv_w, conv_b, w_down, *, layer, tm, tf, rsub, seq):
    m, d = hn.shape
    d_ff = w_down.shape[1]
    nf = d_ff // tf
    kern = functools.partial(_ffn_kernel, tiles_per_seq=seq // tm, rsub=rsub)
    return pl.pallas_call(
        kern,
        grid=(m // tm, nf),
        in_specs=[
            pl.BlockSpec((tm, d), lambda i, f: (i, 0)),
            pl.BlockSpec((None, d, tf), lambda i, f: (layer, 0, f)),
            pl.BlockSpec((None, d, tf), lambda i, f: (layer, 0, nf + f)),
            pl.BlockSpec((CONV_W, tf), lambda i, f: (0, f)),
            pl.BlockSpec((CONV_W, tf), lambda i, f: (0, nf + f)),
            pl.BlockSpec((1, tf), lambda i, f: (0, f)),
            pl.BlockSpec((1, tf), lambda i, f: (0, nf + f)),
            pl.BlockSpec((None, tf, d), lambda i, f: (layer, f, 0)),
        ],
        out_specs=pl.BlockSpec((tm, d), lambda i, f: (i, 0)),
        out_shape=jax.ShapeDtypeStruct((m, d), F32),
        scratch_shapes=[
            pltpu.VMEM((nf, CARRY, 2 * tf), F32),
            pltpu.VMEM((2, CARRY + rsub, 2 * tf), F32),
        ],
        compiler_params=_params(("arbitrary", "arbitrary")),
        name="ffn",
    )(hn, w_up, w_up, conv_w, conv_w, conv_b, conv_b, w_down)


def _ple_kernel(x_ref, dl_ref, p_ref, g_ref, wg_ref, wp_ref, gf_ref, o_ref, *, final):
    x = x_ref[...] + dl_ref[...]
    gate = jax.nn.sigmoid(_dot(_rms(x, g_ref[...]).astype(BF16), wg_ref[...]))
    y = x + gate * _dot(p_ref[...].astype(BF16), wp_ref[...])
    if final:
        y = _rms(y, gf_ref[...])
    o_ref[...] = y


def _ple(x2, delta, p3, g, wg, wp, gf, *, layer, tm, final):
    m, d = x2.shape
    kern = functools.partial(_ple_kernel, final=final)
    return pl.pallas_call(
        kern,
        grid=(m // tm,),
        in_specs=[
            pl.BlockSpec((tm, d), lambda i: (i, 0)),
            pl.BlockSpec((tm, d), lambda i: (i, 0)),
            pl.BlockSpec((None, tm, p3.shape[2]), lambda i: (layer, i, 0)),
            _resident((1, d)),
            _layer(wg.shape[1:], layer),
            _layer(wp.shape[1:], layer),
            _resident((1, d)),
        ],
        out_specs=pl.BlockSpec((tm, d), lambda i: (i, 0)),
        out_shape=jax.ShapeDtypeStruct((m, d), F32),
        compiler_params=_params(("parallel",)),
        name="ple_final" if final else "ple",
    )(x2, delta, p3, g, wg, wp, gf)


def _s5_params(lam_re, lam_im, log_dt, b_re, b_im, c_re, c_im):
    n_groups, n_state = lam_re.shape
    hi = lax.Precision.HIGHEST
    dt = jnp.exp(log_dt)[:, None]
    mag = jnp.exp(lam_re * dt)
    ab_r = mag * jnp.cos(lam_im * dt)
    ab_i = mag * jnp.sin(lam_im * dt)
    den = lam_re * lam_re + lam_im * lam_im
    nr = ab_r - 1.0
    q_r = (nr * lam_re + ab_i * lam_im) / den
    q_i = (ab_i * lam_re - nr * lam_im) / den
    bb_r = q_r[..., None] * b_re - q_i[..., None] * b_im
    bb_i = q_r[..., None] * b_im + q_i[..., None] * b_re
    abb_r = ab_r[..., None] * bb_r - ab_i[..., None] * bb_i
    abb_i = ab_r[..., None] * bb_i + ab_i[..., None] * bb_r
    a2_r = ab_r * ab_r - ab_i * ab_i
    a2_i = 2.0 * ab_r * ab_i
    ca_r = c_re * ab_r[:, None, :] - c_im * ab_i[:, None, :]
    ca_i = c_re * ab_i[:, None, :] + c_im * ab_r[:, None, :]
    ca2_r = c_re * a2_r[:, None, :] - c_im * a2_i[:, None, :]
    ca2_i = c_re * a2_i[:, None, :] + c_im * a2_r[:, None, :]
    k0 = (jnp.einsum('gon,gni->gio', c_re, bb_r, precision=hi)
          - jnp.einsum('gon,gni->gio', c_im, bb_i, precision=hi))
    k1 = (jnp.einsum('gon,gni->gio', c_re, abb_r, precision=hi)
          - jnp.einsum('gon,gni->gio', c_im, abb_i, precision=hi))

    gb = SSM_BLOCK_GROUPS
    n_blocks = n_groups // gb
    eye = jnp.eye(gb, dtype=F32)

    def diag(w):
        p, q = w.shape[1:]
        w = w.reshape(n_blocks, gb, p, q)
        return (w[:, :, :, None, :] * eye[None, :, None, :, None]).reshape(n_blocks, gb * p, gb * q)

    t = lambda w: jnp.swapaxes(w, 1, 2)
    wd = jnp.concatenate([
        jnp.concatenate([diag(t(abb_r)), diag(t(abb_i))], axis=2),
        jnp.concatenate([diag(t(bb_r)), diag(t(bb_i))], axis=2),
    ], axis=1).astype(BF16)
    wr = jnp.concatenate([
        jnp.concatenate([diag(t(ca_r)), diag(t(-ca_i))], axis=1),
        jnp.concatenate([diag(t(ca2_r)), diag(t(-ca2_i))], axis=1),
    ], axis=2).astype(BF16)
    dk0, dk1 = diag(k0), diag(k1)
    wx = jnp.concatenate([
        jnp.concatenate([dk0, dk1], axis=2),
        jnp.concatenate([jnp.zeros_like(dk0), dk0], axis=2),
    ], axis=1).astype(BF16)
    return (wd, wr, wx, a2_r.reshape(n_blocks, 1, gb * n_state), a2_i.reshape(n_blocks, 1, gb * n_state))


def _token_perm(nb_batch, steps):
    dst = jnp.arange(nb_batch * steps)
    b, t = dst // steps, dst % steps
    src = (t % 2) * (steps // 2) * nb_batch + (t // 2) * nb_batch + b
    return (src[:, None] == jnp.arange(nb_batch * steps)[None, :]).astype(BF16)


def _tiles(m, seq):
    tm = min(512, seq)
    tc = min(64, seq)
    tm_ffn = min(1024, seq)
    rsub = min(256, tm_ffn)
    return tm, tc, tm_ffn, rsub


def kernel(x, p, mix_norm, w_in, sgu_norm, sgu_w, sgu_b, s5_lam_re, s5_lam_im, s5_log_dt,
           s5_b_re, s5_b_im, s5_c_re, s5_c_im, s5_d, s5_glu_w, s5_glu_b,
           out_norm_a, out_norm_b, w_out, ffn_norm, ffn_w_up, ffn_conv_w, ffn_conv_b,
           ffn_w_down, ple_norm, ple_w_gate, ple_w_proj, final_norm):
    depth = w_in.shape[0]
    nb_batch, seq, d = x.shape
    m = nb_batch * seq
    d_sgu = sgu_norm.shape[-1]
    d_ssm = s5_d.shape[-1]
    d_ff = ffn_w_down.shape[1]
    n_heads = d_sgu // SGU_HEAD
    assert seq % SGU_CHUNK == 0 and d_sgu % SGU_HEAD == 0
    assert s5_lam_re.shape[1] % SSM_BLOCK_GROUPS == 0
    tm, tc, tm_ffn, rsub = _tiles(m, seq)
    tf = 512 if d_ff % 512 == 0 else d_ff
    assert seq % tm == 0 and tm % SGU_CHUNK == 0 and seq % tc == 0 and tc % 32 == 0
    assert seq % tm_ffn == 0 and tm_ffn % rsub == 0 and rsub % CARRY == 0

    permt = _token_perm(nb_batch, tc // 2)
    tril = jnp.tril(jnp.ones((SGU_CHUNK, SGU_CHUNK), F32))
    row = lambda v: v.reshape(1, -1)

    w_in_b, w_out_b, glu_b16 = w_in.astype(BF16), w_out.astype(BF16), s5_glu_w.astype(BF16)
    w_up_b, w_down_b = ffn_w_up.astype(BF16), ffn_w_down.astype(BF16)
    w_gate_b, w_proj_b = ple_w_gate.astype(BF16), ple_w_proj.astype(BF16)

    x2 = x.reshape(m, d)
    p3 = p.reshape(depth, m, -1)
    for i in range(depth):
        ws = (sgu_w[i] * tril[None]).astype(BF16)
        bs_full = jnp.repeat(jnp.transpose(sgu_b[i]), SGU_HEAD, axis=1)
        ya, xs = _mix_in(x2, row(mix_norm[i]), w_in_b, row(sgu_norm[i]), ws, bs_full,
                         row(out_norm_a[i]), layer=i, tm=tm, tc=tc, seq=seq)
        s5w = _s5_params(s5_lam_re[i], s5_lam_im[i], s5_log_dt[i], s5_b_re[i], s5_b_im[i],
                         s5_c_re[i], s5_c_im[i])
        yb = _s5(xs, permt, *s5w, row(s5_d[i]), glu_b16, row(s5_glu_b[i]),
                 row(out_norm_b[i]), layer=i, tc=tc, nb_batch=nb_batch)
        x2, hn = _mix_out(x2, ya, yb.reshape(m, d_ssm), w_out_b, row(ffn_norm[i]), layer=i, tm=tm)
        delta = _ffn(hn, w_up_b, ffn_conv_w[i], row(ffn_conv_b[i]), w_down_b,
                     layer=i, tm=tm_ffn, tf=tf, rsub=rsub, seq=seq)
        x2 = _ple(x2, delta, p3, row(ple_norm[i]), w_gate_b, w_proj_b, row(final_norm),
                  layer=i, tm=tm, final=(i == depth - 1))
    return x2.reshape(nb_batch, seq, d)
```

```python
import functools
import math

import jax
import jax.numpy as jnp
from jax import lax
from jax.experimental import pallas as pl
from jax.experimental.pallas import tpu as pltpu

EPS = 1e-6
SGU_CHUNK = 128
SGU_HEAD = 128
SSM_GROUP = 16
CONV_W = 3
LANES = 128
SSM_BLOCK_GROUPS = LANES // SSM_GROUP
GATHER_PAD = 8
CARRY = 8
V7X_VMEM_LIMIT_BYTES = 56 * 1024 * 1024

F32 = jnp.float32
BF16 = jnp.bfloat16


def _dot(a, b):
    return jnp.dot(a, b, preferred_element_type=F32)


def _rms(x, g):
    return x * lax.rsqrt(jnp.mean(x * x, axis=-1, keepdims=True) + EPS) * g


def _resident(shape):
    nd = len(shape)
    return pl.BlockSpec(shape, lambda *_: (0,) * nd, pipeline_mode=pl.Buffered(1))


def _layer(shape, layer, block=0):
    nd = len(shape)
    return pl.BlockSpec((None,) + tuple(shape), lambda *_: (layer, block) + (0,) * (nd - 1),
                        pipeline_mode=pl.Buffered(1))


def _params(semantics):
    return pltpu.CompilerParams(dimension_semantics=semantics,
                                vmem_limit_bytes=V7X_VMEM_LIMIT_BYTES)


def _mix_in_kernel(x_ref, gmix_ref, win_ref, gsgu_ref, ws_ref, bs_ref, ga_ref,
                   ya_ref, xs_ref, u_ref, vn_ref, *, d_sgu, n_heads):
    tm = x_ref.shape[0]
    h = _rms(x_ref[...], gmix_ref[...]).astype(BF16)
    u_ref[...] = jax.nn.gelu(_dot(h, win_ref[:, :d_sgu]))
    v = jax.nn.gelu(_dot(h, win_ref[:, d_sgu:2 * d_sgu]))
    xs = _dot(h, win_ref[:, 2 * d_sgu:])
    n_tt, n_slabs, pitch, _ = xs_ref.shape
    tc = pitch - GATHER_PAD
    for j in range(n_slabs):
        xs_ref[:, j, pl.ds(0, tc), :] = xs[:, j * LANES:(j + 1) * LANES].reshape(n_tt, tc, LANES)
        xs_ref[:, j, pl.ds(tc, GATHER_PAD), :] = jnp.zeros((n_tt, GATHER_PAD, LANES), F32)
    vc = v - jnp.mean(v, axis=-1, keepdims=True)
    vn = vc * lax.rsqrt(jnp.mean(vc * vc, axis=-1, keepdims=True) + EPS) * gsgu_ref[...]
    vn_ref[...] = vn.astype(BF16)
    for c in range(tm // SGU_CHUNK):
        rows = pl.ds(c * SGU_CHUNK, SGU_CHUNK)
        for hd in range(n_heads):
            cols = pl.ds(hd * SGU_HEAD, SGU_HEAD)
            sv = _dot(ws_ref[hd], vn_ref[rows, cols]) + bs_ref[:, cols]
            u_ref[rows, cols] = u_ref[rows, cols] * sv
    ya_ref[...] = _rms(u_ref[...], ga_ref[...]).astype(BF16)


def _mix_in(x2, gmix, win, gsgu, ws, bs_full, ga, *, layer, tm, tc, seq):
    m, d = x2.shape
    tiles_per_seq = seq // tm
    d_sgu = gsgu.shape[-1]
    n_slabs = (win.shape[2] - 2 * d_sgu) // LANES
    n_heads = d_sgu // SGU_HEAD
    kern = functools.partial(_mix_in_kernel, d_sgu=d_sgu, n_heads=n_heads)
    return pl.pallas_call(
        kern,
        grid=(m // tm,),
        in_specs=[
            pl.BlockSpec((tm, d), lambda i: (i, 0)),
            _layer((1, d), layer),
            _layer(win.shape[1:], layer),
            _layer((1, d_sgu), layer),
            _layer(ws.shape[1:], layer),
            _layer(bs_full.shape[1:], layer),
            _layer((1, d_sgu), layer),
        ],
        out_specs=[
            pl.BlockSpec((tm, d_sgu), lambda i: (i, 0)),
            pl.BlockSpec((tm // tc, n_slabs, tc + GATHER_PAD, LANES),
                         lambda i: (i % tiles_per_seq, 0, i // tiles_per_seq, 0)),
        ],
        out_shape=[
            jax.ShapeDtypeStruct((m, d_sgu), BF16),
            jax.ShapeDtypeStruct((seq // tc, n_slabs, (m // seq) * (tc + GATHER_PAD), LANES), F32),
        ],
        scratch_shapes=[
            pltpu.VMEM((tm, d_sgu), F32),
            pltpu.VMEM((tm, d_sgu), BF16),
        ],
        compiler_params=_params(("parallel",)),
        name="mix_in",
    )(x2, gmix, win, gsgu, ws, bs_full, ga)


def _s5_kernel(xs_ref, permt_ref, wd_ref, wr_ref, wx_ref, a2r_ref, a2i_ref, d_ref,
               gluw_ref, glub_ref, gb_ref, out_ref,
               h_ref, dr_ref, sp_ref, xtok_ref, xpair_ref, ytok_ref, *, lane_w):
    n_slabs, prow, _ = xs_ref.shape
    nb_batch, tc, d_ssm = out_ref.shape
    pitch = prow // nb_batch
    n_blocks, _, two_n = wd_ref.shape
    n_state = two_n // 2
    n_pairs = tc // 2
    hp = n_pairs // 2
    hrows = hp * nb_batch
    pair_lanes = 2 * LANES

    @pl.when(pl.program_id(0) == 0)
    def _():
        h_ref[...] = jnp.zeros_like(h_ref)

    def tok_rows(half, parity):
        return pl.ds((2 * half + parity) * hrows, hrows)

    for j in range(n_slabs):
        for t in range(tc):
            tau, parity = divmod(t, 2)
            half, tau_h = divmod(tau, hp)
            tile = xs_ref[j, pl.ds(t, nb_batch, stride=pitch), :]
            xtok_ref[pl.ds((2 * half + parity) * hrows + tau_h * nb_batch, nb_batch),
                     pl.ds(j * LANES, LANES)] = tile
            xpair_ref[pl.ds(tau * nb_batch, nb_batch),
                      pl.ds((2 * j + parity) * LANES, LANES)] = tile.astype(BF16)

    def drive(blk):
        dr_ref[blk % 2] = _dot(xpair_ref[:, pl.ds(blk * pair_lanes, pair_lanes)], wd_ref[blk])

    def scan(blk):
        buf = dr_ref.at[blk % 2]
        sp = sp_ref.at[blk % 2]
        for q in range(n_state // lane_w):
            re = pl.ds(q * lane_w, lane_w)
            im = pl.ds(n_state + q * lane_w, lane_w)
            a_r = jnp.broadcast_to(a2r_ref[blk, :, re], (nb_batch, lane_w))
            a_i = jnp.broadcast_to(a2i_ref[blk, :, re], (nb_batch, lane_w))
            h_r, h_i = h_ref[blk, :, re], h_ref[blk, :, im]
            for tau in range(n_pairs):
                r = pl.ds(tau * nb_batch, nb_batch)
                sp[r, re] = h_r.astype(BF16)
                sp[r, im] = h_i.astype(BF16)
                h_r, h_i = (a_r * h_r - a_i * h_i + buf[r, re],
                            a_r * h_i + a_i * h_r + buf[r, im])
            h_ref[blk, :, re] = h_r
            h_ref[blk, :, im] = h_i

    def readout(blk):
        yy = (_dot(sp_ref[blk % 2], wr_ref[blk])
              + _dot(xpair_ref[:, pl.ds(blk * pair_lanes, pair_lanes)], wx_ref[blk]))
        ch = pl.ds(blk * LANES, LANES)
        for half in range(2):
            for parity in range(2):
                dst = tok_rows(half, parity)
                ytok_ref[dst, ch] = (yy[half * hrows:(half + 1) * hrows, parity * LANES:(parity + 1) * LANES]
                                     + d_ref[:, ch] * xtok_ref[dst, ch])

    drive(0)
    for blk in range(n_blocks):
        if blk + 1 < n_blocks:
            drive(blk + 1)
        scan(blk)
        readout(blk)

    z = jax.nn.gelu(ytok_ref[...])
    z = z * jax.nn.sigmoid(_dot(z.astype(BF16), gluw_ref[...]) + glub_ref[...])
    zn = _rms(z, gb_ref[...]).astype(BF16)
    for half in range(2):
        out_ref[:, pl.ds(half * (tc // 2), tc // 2), :] = (
            _dot(permt_ref[...], zn[2 * half * hrows:2 * (half + 1) * hrows])
            .astype(BF16).reshape(nb_batch, tc // 2, d_ssm))


def _s5(xs4, permt, wd, wr, wx, a2_r, a2_i, dskip, gluw, glub, gb, *, layer, tc, nb_batch):
    n_tiles, n_slabs, prow, _ = xs4.shape
    d_ssm = n_slabs * LANES
    seq = n_tiles * tc
    rows = nb_batch * tc
    _, n_blocks, _, two_n = wd.shape
    assert n_blocks == n_slabs and prow == nb_batch * (tc + GATHER_PAD)
    kern = functools.partial(_s5_kernel, lane_w=256)
    return pl.pallas_call(
        kern,
        grid=(n_tiles,),
        in_specs=[
            pl.BlockSpec((None, n_slabs, prow, LANES), lambda i: (i, 0, 0, 0)),
            _resident(permt.shape),
            _layer(wd.shape[1:], layer),
            _layer(wr.shape[1:], layer),
            _layer(wx.shape[1:], layer),
            _layer(a2_r.shape[1:], layer),
            _layer(a2_i.shape[1:], layer),
            _layer((1, d_ssm), layer),
            _layer(gluw.shape[1:], layer),
            _layer((1, d_ssm), layer),
            _layer((1, d_ssm), layer),
        ],
        out_specs=pl.BlockSpec((nb_batch, tc, d_ssm), lambda i: (0, i, 0)),
        out_shape=jax.ShapeDtypeStruct((nb_batch, seq, d_ssm), BF16),
        scratch_shapes=[
            pltpu.VMEM((n_blocks, nb_batch, two_n), F32),
            pltpu.VMEM((2, rows // 2, two_n), F32),
            pltpu.VMEM((2, rows // 2, two_n), BF16),
            pltpu.VMEM((rows, d_ssm), F32),
            pltpu.VMEM((rows // 2, 2 * d_ssm), BF16),
            pltpu.VMEM((rows, d_ssm), F32),
        ],
        compiler_params=_params(("arbitrary",)),
        name="s5",
    )(xs4, permt, wd, wr, wx, a2_r, a2_i, dskip, gluw, glub, gb)


def _mix_out_kernel(x_ref, ya_ref, yb_ref, wa_ref, wb_ref, g_ref, o_ref, hn_ref):
    x1 = x_ref[...] + _dot(ya_ref[...], wa_ref[...]) + _dot(yb_ref[...], wb_ref[...])
    o_ref[...] = x1
    hn_ref[...] = _rms(x1, g_ref[...]).astype(BF16)


def _mix_out(x2, ya, yb, wo, g, *, layer, tm):
    m, d = x2.shape
    return pl.pallas_call(
        _mix_out_kernel,
        grid=(m // tm,),
        in_specs=[
            pl.BlockSpec((tm, d), lambda i: (i, 0)),
            pl.BlockSpec((tm, ya.shape[1]), lambda i: (i, 0)),
            pl.BlockSpec((tm, yb.shape[1]), lambda i: (i, 0)),
            _layer((ya.shape[1], d), layer, 0),
            _layer((yb.shape[1], d), layer, 1),
            _layer((1, d), layer),
        ],
        out_specs=[
            pl.BlockSpec((tm, d), lambda i: (i, 0)),
            pl.BlockSpec((tm, d), lambda i: (i, 0)),
        ],
        out_shape=[
            jax.ShapeDtypeStruct((m, d), F32),
            jax.ShapeDtypeStruct((m, d), BF16),
        ],
        compiler_params=_params(("parallel",)),
        name="mix_out",
    )(x2, ya, yb, wo, wo, g)


def _ffn_kernel(hn_ref, wg_ref, wu_ref, cwg_ref, cwu_ref, cbg_ref, cbu_ref, wd_ref, o_ref,
                carry_ref, a_ref, *, tiles_per_seq, rsub):
    i = pl.program_id(0)
    f = pl.program_id(1)
    tm = hn_ref.shape[0]
    tf = wd_ref.shape[0]
    n_sub = tm // rsub

    @pl.when(f == 0)
    def _():
        o_ref[...] = jnp.zeros_like(o_ref)

    @pl.when(i % tiles_per_seq == 0)
    def _():
        carry_ref[f] = jnp.zeros(carry_ref.shape[1:], F32)

    a_ref[0, pl.ds(0, CARRY), :] = carry_ref[f]

    def project(r):
        h = hn_ref[pl.ds(r * rsub, rsub), :]
        a_ref[r % 2, pl.ds(CARRY, rsub), pl.ds(0, tf)] = _dot(h, wg_ref[...])
        a_ref[r % 2, pl.ds(CARRY, rsub), pl.ds(tf, tf)] = _dot(h, wu_ref[...])

    def conv(slot, cols, cw_ref, cb_ref):
        return (cw_ref[0:1, :] * a_ref[slot, pl.ds(CARRY - 2, rsub), cols]
                + cw_ref[1:2, :] * a_ref[slot, pl.ds(CARRY - 1, rsub), cols]
                + cw_ref[2:3, :] * a_ref[slot, pl.ds(CARRY, rsub), cols] + cb_ref[...])

    project(0)
    for r in range(n_sub):
        slot = r % 2
        if r + 1 < n_sub:
            project(r + 1)
        gate = conv(slot, pl.ds(0, tf), cwg_ref, cbg_ref)
        up = conv(slot, pl.ds(tf, tf), cwu_ref, cbu_ref)
        tail = a_ref[slot, pl.ds(rsub, CARRY), :]
        if r + 1 < n_sub:
            a_ref[1 - slot, pl.ds(0, CARRY), :] = tail
        else:
            carry_ref[f] = tail
        act = (jax.nn.silu(gate) * up).astype(BF16)
        o_ref[pl.ds(r * rsub, rsub), :] += _dot(act, wd_ref[...])


def _ffn(hn, w_up, conv_w, conv_b, w_down, *, layer, tm, tf, rsub, seq):
    m, d = hn.shape
    d_ff = w_down.shape[1]
    nf = d_ff // tf
    kern = functools.partial(_ffn_kernel, tiles_per_seq=seq // tm, rsub=rsub)
    return pl.pallas_call(
        kern,
        grid=(m // tm, nf),
        in_specs=[
            pl.BlockSpec((tm, d), lambda i, f: (i, 0)),
            pl.BlockSpec((None, d, tf), lambda i, f: (layer, 0, f)),
            pl.BlockSpec((None, d, tf), lambda i, f: (layer, 0, nf + f)),
            pl.BlockSpec((None, CONV_W, tf), lambda i, f: (layer, 0, f)),
            pl.BlockSpec((None, CONV_W, tf), lambda i, f: (layer, 0, nf + f)),
            pl.BlockSpec((None, 1, tf), lambda i, f: (layer, 0, f)),
            pl.BlockSpec((None, 1, tf), lambda i, f: (layer, 0, nf + f)),
            pl.BlockSpec((None, tf, d), lambda i, f: (layer, f, 0)),
        ],
        out_specs=pl.BlockSpec((tm, d), lambda i, f: (i, 0)),
        out_shape=jax.ShapeDtypeStruct((m, d), F32),
        scratch_shapes=[
            pltpu.VMEM((nf, CARRY, 2 * tf), F32),
            pltpu.VMEM((2, CARRY + rsub, 2 * tf), F32),
        ],
        compiler_params=_params(("arbitrary", "arbitrary")),
        name="ffn",
    )(hn, w_up, w_up, conv_w, conv_w, conv_b, conv_b, w_down)


def _ple_kernel(x_ref, dl_ref, p_ref, g_ref, wg_ref, wp_ref, gf_ref, o_ref, *, final):
    x = x_ref[...] + dl_ref[...]
    gate = jax.nn.sigmoid(_dot(_rms(x, g_ref[...]).astype(BF16), wg_ref[...]))
    y = x + gate * _dot(p_ref[...].astype(BF16), wp_ref[...])
    if final:
        y = _rms(y, gf_ref[...])
    o_ref[...] = y


def _ple(x2, delta, p3, g, wg, wp, gf, *, layer, tm, final):
    m, d = x2.shape
    kern = functools.partial(_ple_kernel, final=final)
    return pl.pallas_call(
        kern,
        grid=(m // tm,),
        in_specs=[
            pl.BlockSpec((tm, d), lambda i: (i, 0)),
            pl.BlockSpec((tm, d), lambda i: (i, 0)),
            pl.BlockSpec((None, tm, p3.shape[2]), lambda i: (layer, i, 0)),
            _layer((1, d), layer),
            _layer(wg.shape[1:], layer),
            _layer(wp.shape[1:], layer),
            _resident((1, d)),
        ],
        out_specs=pl.BlockSpec((tm, d), lambda i: (i, 0)),
        out_shape=jax.ShapeDtypeStruct((m, d), F32),
        compiler_params=_params(("parallel",)),
        name="ple_final" if final else "ple",
    )(x2, delta, p3, g, wg, wp, gf)


def _s5_params(lam_re, lam_im, log_dt, b_re, b_im, c_re, c_im):
    n_groups, n_state = lam_re.shape
    hi = lax.Precision.HIGHEST
    dt = jnp.exp(log_dt)[:, None]
    mag = jnp.exp(lam_re * dt)
    ab_r = mag * jnp.cos(lam_im * dt)
    ab_i = mag * jnp.sin(lam_im * dt)
    den = lam_re * lam_re + lam_im * lam_im
    nr = ab_r - 1.0
    q_r = (nr * lam_re + ab_i * lam_im) / den
    q_i = (ab_i * lam_re - nr * lam_im) / den
    bb_r = q_r[..., None] * b_re - q_i[..., None] * b_im
    bb_i = q_r[..., None] * b_im + q_i[..., None] * b_re
    abb_r = ab_r[..., None] * bb_r - ab_i[..., None] * bb_i
    abb_i = ab_r[..., None] * bb_i + ab_i[..., None] * bb_r
    a2_r = ab_r * ab_r - ab_i * ab_i
    a2_i = 2.0 * ab_r * ab_i
    ca_r = c_re * ab_r[:, None, :] - c_im * ab_i[:, None, :]
    ca_i = c_re * ab_i[:, None, :] + c_im * ab_r[:, None, :]
    ca2_r = c_re * a2_r[:, None, :] - c_im * a2_i[:, None, :]
    ca2_i = c_re * a2_i[:, None, :] + c_im * a2_r[:, None, :]
    k0 = (jnp.einsum('gon,gni->gio', c_re, bb_r, precision=hi)
          - jnp.einsum('gon,gni->gio', c_im, bb_i, precision=hi))
    k1 = (jnp.einsum('gon,gni->gio', c_re, abb_r, precision=hi)
          - jnp.einsum('gon,gni->gio', c_im, abb_i, precision=hi))

    gb = SSM_BLOCK_GROUPS
    n_blocks = n_groups // gb
    eye = jnp.eye(gb, dtype=F32)

    def diag(w):
        p, q = w.shape[1:]
        w = w.reshape(n_blocks, gb, p, q)
        return (w[:, :, :, None, :] * eye[None, :, None, :, None]).reshape(n_blocks, gb * p, gb * q)

    t = lambda w: jnp.swapaxes(w, 1, 2)
    wd = jnp.concatenate([
        jnp.concatenate([diag(t(abb_r)), diag(t(abb_i))], axis=2),
        jnp.concatenate([diag(t(bb_r)), diag(t(bb_i))], axis=2),
    ], axis=1).astype(BF16)
    wr = jnp.concatenate([
        jnp.concatenate([diag(t(ca_r)), diag(t(-ca_i))], axis=1),
        jnp.concatenate([diag(t(ca2_r)), diag(t(-ca2_i))], axis=1),
    ], axis=2).astype(BF16)
    dk0, dk1 = diag(k0), diag(k1)
    wx = jnp.concatenate([
        jnp.concatenate([dk0, dk1], axis=2),
        jnp.concatenate([jnp.zeros_like(dk0), dk0], axis=2),
    ], axis=1).astype(BF16)
    return (wd, wr, wx, a2_r.reshape(n_blocks, 1, gb * n_state), a2_i.reshape(n_blocks, 1, gb * n_state))


def _token_perm(nb_batch, steps):
    dst = jnp.arange(nb_batch * steps)
    b, t = dst // steps, dst % steps
    src = (t % 2) * (steps // 2) * nb_batch + (t // 2) * nb_batch + b
    return (src[:, None] == jnp.arange(nb_batch * steps)[None, :]).astype(BF16)


def _tiles(m, seq):
    tm = min(512, seq)
    tc = min(64, seq)
    tm_ffn = min(1024, seq)
    rsub = min(256, tm_ffn)
    return tm, tc, tm_ffn, rsub


def kernel(x, p, mix_norm, w_in, sgu_norm, sgu_w, sgu_b, s5_lam_re, s5_lam_im, s5_log_dt,
           s5_b_re, s5_b_im, s5_c_re, s5_c_im, s5_d, s5_glu_w, s5_glu_b,
           out_norm_a, out_norm_b, w_out, ffn_norm, ffn_w_up, ffn_conv_w, ffn_conv_b,
           ffn_w_down, ple_norm, ple_w_gate, ple_w_proj, final_norm):
    depth = w_in.shape[0]
    nb_batch, seq, d = x.shape
    m = nb_batch * seq
    d_sgu = sgu_norm.shape[-1]
    d_ssm = s5_d.shape[-1]
    d_ff = ffn_w_down.shape[1]
    n_heads = d_sgu // SGU_HEAD
    assert seq % SGU_CHUNK == 0 and d_sgu % SGU_HEAD == 0
    assert s5_lam_re.shape[1] % SSM_BLOCK_GROUPS == 0
    tm, tc, tm_ffn, rsub = _tiles(m, seq)
    tf = 512 if d_ff % 512 == 0 else d_ff
    assert seq % tm == 0 and tm % SGU_CHUNK == 0 and seq % tc == 0 and tc % 32 == 0
    assert seq % tm_ffn == 0 and tm_ffn % rsub == 0 and rsub % CARRY == 0

    rows = lambda v: v.reshape(v.shape[0], 1, -1)
    w_in_b, w_out_b, glu_b16 = w_in.astype(BF16), w_out.astype(BF16), s5_glu_w.astype(BF16)
    w_up_b, w_down_b = ffn_w_up.astype(BF16), ffn_w_down.astype(BF16)
    w_gate_b, w_proj_b = ple_w_gate.astype(BF16), ple_w_proj.astype(BF16)
    tril = jnp.tril(jnp.ones((SGU_CHUNK, SGU_CHUNK), F32))
    ws = (sgu_w * tril).astype(BF16)
    bs_full = jnp.repeat(jnp.swapaxes(sgu_b, 1, 2), SGU_HEAD, axis=2)
    s5w = jax.vmap(_s5_params)(s5_lam_re, s5_lam_im, s5_log_dt, s5_b_re, s5_b_im, s5_c_re, s5_c_im)
    permt = _token_perm(nb_batch, tc // 2)
    gmix, gsgu, ga, gbn = rows(mix_norm), rows(sgu_norm), rows(out_norm_a), rows(out_norm_b)
    dskip, glub, gffn, gple = rows(s5_d), rows(s5_glu_b), rows(ffn_norm), rows(ple_norm)
    conv_b = rows(ffn_conv_b)
    gfinal = final_norm.reshape(1, -1)

    x2 = x.reshape(m, d)
    p3 = p.reshape(depth, m, -1)
    for i in range(depth):
        ya, xs = _mix_in(x2, gmix, w_in_b, gsgu, ws, bs_full, ga, layer=i, tm=tm, tc=tc, seq=seq)
        yb = _s5(xs, permt, *s5w, dskip, glu_b16, glub, gbn, layer=i, tc=tc, nb_batch=nb_batch)
        x2, hn = _mix_out(x2, ya, yb.reshape(m, d_ssm), w_out_b, gffn, layer=i, tm=tm)
        delta = _ffn(hn, w_up_b, ffn_conv_w, conv_b, w_down_b,
                     layer=i, tm=tm_ffn, tf=tf, rsub=rsub, seq=seq)
        x2 = _ple(x2, delta, p3, gple, w_gate_b, w_proj_b, gfinal,
                  layer=i, tm=tm, final=(i == depth - 1))
    return x2.reshape(nb_batch, seq, d)
```

```python
import functools
import math

import jax
import jax.numpy as jnp
from jax import lax
from jax.experimental import pallas as pl
from jax.experimental.pallas import tpu as pltpu

EPS = 1e-6
SGU_CHUNK = 128
SGU_HEAD = 128
SSM_GROUP = 16
CONV_W = 3
LANES = 128
SSM_BLOCK_GROUPS = LANES // SSM_GROUP
GATHER_PAD = 8
CARRY = 8
V7X_VMEM_LIMIT_BYTES = 56 * 1024 * 1024

F32 = jnp.float32
BF16 = jnp.bfloat16


def _dot(a, b):
    return jnp.dot(a, b, preferred_element_type=F32)


def _rms(x, g):
    return x * lax.rsqrt(jnp.mean(x * x, axis=-1, keepdims=True) + EPS) * g


def _resident(shape):
    nd = len(shape)
    return pl.BlockSpec(shape, lambda *_: (0,) * nd, pipeline_mode=pl.Buffered(1))


def _layer(shape, layer, block=0):
    nd = len(shape)
    return pl.BlockSpec((None,) + tuple(shape), lambda *_: (layer, block) + (0,) * (nd - 1),
                        pipeline_mode=pl.Buffered(1))


def _params(semantics):
    return pltpu.CompilerParams(dimension_semantics=semantics,
                                vmem_limit_bytes=V7X_VMEM_LIMIT_BYTES)


def _mix_in_kernel(x_ref, gmix_ref, win_ref, gsgu_ref, ws_ref, bs_ref, ga_ref,
                   ya_ref, xs_ref, u_ref, vn_ref, *, d_sgu, n_heads):
    tm = x_ref.shape[0]
    h = _rms(x_ref[...], gmix_ref[...]).astype(BF16)
    u_ref[...] = jax.nn.gelu(_dot(h, win_ref[:, :d_sgu]))
    v = jax.nn.gelu(_dot(h, win_ref[:, d_sgu:2 * d_sgu]))
    xs = _dot(h, win_ref[:, 2 * d_sgu:])
    n_tt, n_slabs, pitch, _ = xs_ref.shape
    tc = pitch - GATHER_PAD
    for j in range(n_slabs):
        xs_ref[:, j, pl.ds(0, tc), :] = xs[:, j * LANES:(j + 1) * LANES].reshape(n_tt, tc, LANES)
        xs_ref[:, j, pl.ds(tc, GATHER_PAD), :] = jnp.zeros((n_tt, GATHER_PAD, LANES), F32)
    vc = v - jnp.mean(v, axis=-1, keepdims=True)
    vn = vc * lax.rsqrt(jnp.mean(vc * vc, axis=-1, keepdims=True) + EPS) * gsgu_ref[...]
    vn_ref[...] = vn.astype(BF16)
    for c in range(tm // SGU_CHUNK):
        rows = pl.ds(c * SGU_CHUNK, SGU_CHUNK)
        for hd in range(n_heads):
            cols = pl.ds(hd * SGU_HEAD, SGU_HEAD)
            sv = _dot(ws_ref[hd], vn_ref[rows, cols]) + bs_ref[:, cols]
            u_ref[rows, cols] = u_ref[rows, cols] * sv
    ya_ref[...] = _rms(u_ref[...], ga_ref[...]).astype(BF16)


def _mix_in(x2, gmix, win, gsgu, ws, bs_full, ga, *, layer, tm, tc, seq):
    m, d = x2.shape
    tiles_per_seq = seq // tm
    d_sgu = gsgu.shape[-1]
    n_slabs = (win.shape[2] - 2 * d_sgu) // LANES
    n_heads = d_sgu // SGU_HEAD
    kern = functools.partial(_mix_in_kernel, d_sgu=d_sgu, n_heads=n_heads)
    return pl.pallas_call(
        kern,
        grid=(m // tm,),
        in_specs=[
            pl.BlockSpec((tm, d), lambda i: (i, 0)),
            _resident((1, d)),
            _layer(win.shape[1:], layer),
            _resident((1, d_sgu)),
            _resident(ws.shape),
            _resident(bs_full.shape),
            _resident((1, d_sgu)),
        ],
        out_specs=[
            pl.BlockSpec((tm, d_sgu), lambda i: (i, 0)),
            pl.BlockSpec((tm // tc, n_slabs, tc + GATHER_PAD, LANES),
                         lambda i: (i % tiles_per_seq, 0, i // tiles_per_seq, 0)),
        ],
        out_shape=[
            jax.ShapeDtypeStruct((m, d_sgu), BF16),
            jax.ShapeDtypeStruct((seq // tc, n_slabs, (m // seq) * (tc + GATHER_PAD), LANES), F32),
        ],
        scratch_shapes=[
            pltpu.VMEM((tm, d_sgu), F32),
            pltpu.VMEM((tm, d_sgu), BF16),
        ],
        compiler_params=_params(("parallel",)),
        name="mix_in",
    )(x2, gmix, win, gsgu, ws, bs_full, ga)


def _s5_kernel(xs_ref, permt_ref, wd_ref, wr_ref, wx_ref, a2r_ref, a2i_ref, d_ref,
               gluw_ref, glub_ref, gb_ref, out_ref,
               h_ref, dr_ref, sp_ref, xtok_ref, xpair_ref, ytok_ref, *, lane_w):
    n_slabs, prow, _ = xs_ref.shape
    nb_batch, tc, d_ssm = out_ref.shape
    pitch = prow // nb_batch
    n_blocks, _, two_n = wd_ref.shape
    n_state = two_n // 2
    n_pairs = tc // 2
    hp = n_pairs // 2
    hrows = hp * nb_batch
    pair_lanes = 2 * LANES

    @pl.when(pl.program_id(0) == 0)
    def _():
        h_ref[...] = jnp.zeros_like(h_ref)

    def tok_rows(half, parity):
        return pl.ds((2 * half + parity) * hrows, hrows)

    for j in range(n_slabs):
        for t in range(tc):
            tau, parity = divmod(t, 2)
            half, tau_h = divmod(tau, hp)
            tile = xs_ref[j, pl.ds(t, nb_batch, stride=pitch), :]
            xtok_ref[pl.ds((2 * half + parity) * hrows + tau_h * nb_batch, nb_batch),
                     pl.ds(j * LANES, LANES)] = tile
            xpair_ref[pl.ds(tau * nb_batch, nb_batch),
                      pl.ds((2 * j + parity) * LANES, LANES)] = tile.astype(BF16)

    def drive(blk):
        dr_ref[blk % 2] = _dot(xpair_ref[:, pl.ds(blk * pair_lanes, pair_lanes)], wd_ref[blk])

    def scan(blk):
        buf = dr_ref.at[blk % 2]
        sp = sp_ref.at[blk % 2]
        for q in range(n_state // lane_w):
            re = pl.ds(q * lane_w, lane_w)
            im = pl.ds(n_state + q * lane_w, lane_w)
            a_r = jnp.broadcast_to(a2r_ref[blk, :, re], (nb_batch, lane_w))
            a_i = jnp.broadcast_to(a2i_ref[blk, :, re], (nb_batch, lane_w))
            h_r, h_i = h_ref[blk, :, re], h_ref[blk, :, im]
            for tau in range(n_pairs):
                r = pl.ds(tau * nb_batch, nb_batch)
                sp[r, re] = h_r.astype(BF16)
                sp[r, im] = h_i.astype(BF16)
                h_r, h_i = (a_r * h_r - a_i * h_i + buf[r, re],
                            a_r * h_i + a_i * h_r + buf[r, im])
            h_ref[blk, :, re] = h_r
            h_ref[blk, :, im] = h_i

    def readout(blk):
        yy = (_dot(sp_ref[blk % 2], wr_ref[blk])
              + _dot(xpair_ref[:, pl.ds(blk * pair_lanes, pair_lanes)], wx_ref[blk]))
        ch = pl.ds(blk * LANES, LANES)
        for half in range(2):
            for parity in range(2):
                dst = tok_rows(half, parity)
                ytok_ref[dst, ch] = (yy[half * hrows:(half + 1) * hrows, parity * LANES:(parity + 1) * LANES]
                                     + d_ref[:, ch] * xtok_ref[dst, ch])

    drive(0)
    for blk in range(n_blocks):
        if blk + 1 < n_blocks:
            drive(blk + 1)
        scan(blk)
        readout(blk)

    z = jax.nn.gelu(ytok_ref[...])
    z = z * jax.nn.sigmoid(_dot(z.astype(BF16), gluw_ref[...]) + glub_ref[...])
    zn = _rms(z, gb_ref[...]).astype(BF16)
    for half in range(2):
        out_ref[:, pl.ds(half * (tc // 2), tc // 2), :] = (
            _dot(permt_ref[...], zn[2 * half * hrows:2 * (half + 1) * hrows])
            .astype(BF16).reshape(nb_batch, tc // 2, d_ssm))


def _s5(xs4, permt, wd, wr, wx, a2_r, a2_i, dskip, gluw, glub, gb, *, layer, tc, nb_batch):
    n_tiles, n_slabs, prow, _ = xs4.shape
    d_ssm = n_slabs * LANES
    seq = n_tiles * tc
    rows = nb_batch * tc
    n_blocks, _, two_n = wd.shape
    assert n_blocks == n_slabs and prow == nb_batch * (tc + GATHER_PAD)
    kern = functools.partial(_s5_kernel, lane_w=256)
    return pl.pallas_call(
        kern,
        grid=(n_tiles,),
        in_specs=[
            pl.BlockSpec((None, n_slabs, prow, LANES), lambda i: (i, 0, 0, 0)),
            _resident(permt.shape),
            _resident(wd.shape),
            _resident(wr.shape),
            _resident(wx.shape),
            _resident(a2_r.shape),
            _resident(a2_i.shape),
            _resident((1, d_ssm)),
            _layer(gluw.shape[1:], layer),
            _resident((1, d_ssm)),
            _resident((1, d_ssm)),
        ],
        out_specs=pl.BlockSpec((nb_batch, tc, d_ssm), lambda i: (0, i, 0)),
        out_shape=jax.ShapeDtypeStruct((nb_batch, seq, d_ssm), BF16),
        scratch_shapes=[
            pltpu.VMEM((n_blocks, nb_batch, two_n), F32),
            pltpu.VMEM((2, rows // 2, two_n), F32),
            pltpu.VMEM((2, rows // 2, two_n), BF16),
            pltpu.VMEM((rows, d_ssm), F32),
            pltpu.VMEM((rows // 2, 2 * d_ssm), BF16),
            pltpu.VMEM((rows, d_ssm), F32),
        ],
        compiler_params=_params(("arbitrary",)),
        name="s5",
    )(xs4, permt, wd, wr, wx, a2_r, a2_i, dskip, gluw, glub, gb)


def _mix_out_kernel(x_ref, ya_ref, yb_ref, wa_ref, wb_ref, g_ref, o_ref, hn_ref):
    x1 = x_ref[...] + _dot(ya_ref[...], wa_ref[...]) + _dot(yb_ref[...], wb_ref[...])
    o_ref[...] = x1
    hn_ref[...] = _rms(x1, g_ref[...]).astype(BF16)


def _mix_out(x2, ya, yb, wo, g, *, layer, tm):
    m, d = x2.shape
    return pl.pallas_call(
        _mix_out_kernel,
        grid=(m // tm,),
        in_specs=[
            pl.BlockSpec((tm, d), lambda i: (i, 0)),
            pl.BlockSpec((tm, ya.shape[1]), lambda i: (i, 0)),
            pl.BlockSpec((tm, yb.shape[1]), lambda i: (i, 0)),
            _layer((ya.shape[1], d), layer, 0),
            _layer((yb.shape[1], d), layer, 1),
            _resident((1, d)),
        ],
        out_specs=[
            pl.BlockSpec((tm, d), lambda i: (i, 0)),
            pl.BlockSpec((tm, d), lambda i: (i, 0)),
        ],
        out_shape=[
            jax.ShapeDtypeStruct((m, d), F32),
            jax.ShapeDtypeStruct((m, d), BF16),
        ],
        compiler_params=_params(("parallel",)),
        name="mix_out",
    )(x2, ya, yb, wo, wo, g)


def _ffn_kernel(hn_ref, wg_ref, wu_ref, cwg_ref, cwu_ref, cbg_ref, cbu_ref, wd_ref, o_ref,
                carry_ref, a_ref, *, tiles_per_seq, rsub):
    i = pl.program_id(0)
    f = pl.program_id(1)
    tm = hn_ref.shape[0]
    tf = wd_ref.shape[0]
    n_sub = tm // rsub

    @pl.when(f == 0)
    def _():
        o_ref[...] = jnp.zeros_like(o_ref)

    @pl.when(i % tiles_per_seq == 0)
    def _():
        carry_ref[f] = jnp.zeros(carry_ref.shape[1:], F32)

    a_ref[0, pl.ds(0, CARRY), :] = carry_ref[f]

    def project(r):
        h = hn_ref[pl.ds(r * rsub, rsub), :]
        a_ref[r % 2, pl.ds(CARRY, rsub), pl.ds(0, tf)] = _dot(h, wg_ref[...])
        a_ref[r % 2, pl.ds(CARRY, rsub), pl.ds(tf, tf)] = _dot(h, wu_ref[...])

    def conv(slot, cols, cw_ref, cb_ref):
        return (cw_ref[0:1, :] * a_ref[slot, pl.ds(CARRY - 2, rsub), cols]
                + cw_ref[1:2, :] * a_ref[slot, pl.ds(CARRY - 1, rsub), cols]
                + cw_ref[2:3, :] * a_ref[slot, pl.ds(CARRY, rsub), cols] + cb_ref[...])

    project(0)
    for r in range(n_sub):
        slot = r % 2
        if r + 1 < n_sub:
            project(r + 1)
        gate = conv(slot, pl.ds(0, tf), cwg_ref, cbg_ref)
        up = conv(slot, pl.ds(tf, tf), cwu_ref, cbu_ref)
        tail = a_ref[slot, pl.ds(rsub, CARRY), :]
        if r + 1 < n_sub:
            a_ref[1 - slot, pl.ds(0, CARRY), :] = tail
        else:
            carry_ref[f] = tail
        act = (jax.nn.silu(gate) * up).astype(BF16)
        o_ref[pl.ds(r * rsub, rsub), :] += _dot(act, wd_ref[...])


def _ffn(hn, w_up, conv_w, conv_b, w_down, *, layer, tm, tf, rsub, seq):
    m, d = hn.shape
    d_ff = w_down.shape[1]
    nf = d_ff // tf
    kern = functools.partial(_ffn_kernel, tiles_per_seq=seq // tm, rsub=rsub)
    return pl.pallas_call(
        kern,
        grid=(m // tm, nf),
        in_specs=[
            pl.BlockSpec((tm, d), lambda i, f: (i, 0)),
            pl.BlockSpec((None, d, tf), lambda i, f: (layer, 0, f)),
            pl.BlockSpec((None, d, tf), lambda i, f: (layer, 0, nf + f)),
            pl.BlockSpec((CONV_W, tf), lambda i, f: (0, f)),
            pl.BlockSpec((CONV_W, tf), lambda i, f: (0, nf + f)),
            pl.BlockSpec((1, tf), lambda i, f: (0, f)),
            pl.BlockSpec((1, tf), lambda i, f: (0, nf + f)),
            pl.BlockSpec((None, tf, d), lambda i, f: (layer, f, 0)),
        ],
        out_specs=pl.BlockSpec((tm, d), lambda i, f: (i, 0)),
        out_shape=jax.ShapeDtypeStruct((m, d), F32),
        scratch_shapes=[
            pltpu.VMEM((nf, CARRY, 2 * tf), F32),
            pltpu.VMEM((2, CARRY + rsub, 2 * tf), F32),
        ],
        compiler_params=_params(("arbitrary", "arbitrary")),
        name="ffn",
    )(hn, w_up, w_up, conv_w, conv_w, conv_b, conv_b, w_down)


def _ple_kernel(x_ref, dl_ref, p_ref, g_ref, wg_ref, wp_ref, gf_ref, o_ref, *, final):
    x = x_ref[...] + dl_ref[...]
    gate = jax.nn.sigmoid(_dot(_rms(x, g_ref[...]).astype(BF16), wg_ref[...]))
    y = x + gate * _dot(p_ref[...].astype(BF16), wp_ref[...])
    if final:
        y = _rms(y, gf_ref[...])
    o_ref[...] = y


def _ple(x2, delta, p3, g, wg, wp, gf, *, layer, tm, final):
    m, d = x2.shape
    kern = functools.partial(_ple_kernel, final=final)
    return pl.pallas_call(
        kern,
        grid=(m // tm,),
        in_specs=[
            pl.BlockSpec((tm, d), lambda i: (i, 0)),
            pl.BlockSpec((tm, d), lambda i: (i, 0)),
            pl.BlockSpec((None, tm, p3.shape[2]), lambda i: (layer, i, 0)),
            _resident((1, d)),
            _layer(wg.shape[1:], layer),
            _layer(wp.shape[1:], layer),
            _resident((1, d)),
        ],
        out_specs=pl.BlockSpec((tm, d), lambda i: (i, 0)),
        out_shape=jax.ShapeDtypeStruct((m, d), F32),
        compiler_params=_params(("parallel",)),
        name="ple_final" if final else "ple",
    )(x2, delta, p3, g, wg, wp, gf)


def _s5_params(lam_re, lam_im, log_dt, b_re, b_im, c_re, c_im):
    n_groups, n_state = lam_re.shape
    hi = lax.Precision.HIGHEST
    dt = jnp.exp(log_dt)[:, None]
    mag = jnp.exp(lam_re * dt)
    ab_r = mag * jnp.cos(lam_im * dt)
    ab_i = mag * jnp.sin(lam_im * dt)
    den = lam_re * lam_re + lam_im * lam_im
    nr = ab_r - 1.0
    q_r = (nr * lam_re + ab_i * lam_im) / den
    q_i = (ab_i * lam_re - nr * lam_im) / den
    bb_r = q_r[..., None] * b_re - q_i[..., None] * b_im
    bb_i = q_r[..., None] * b_im + q_i[..., None] * b_re
    abb_r = ab_r[..., None] * bb_r - ab_i[..., None] * bb_i
    abb_i = ab_r[..., None] * bb_i + ab_i[..., None] * bb_r
    a2_r = ab_r * ab_r - ab_i * ab_i
    a2_i = 2.0 * ab_r * ab_i
    ca_r = c_re * ab_r[:, None, :] - c_im * ab_i[:, None, :]
    ca_i = c_re * ab_i[:, None, :] + c_im * ab_r[:, None, :]
    ca2_r = c_re * a2_r[:, None, :] - c_im * a2_i[:, None, :]
    ca2_i = c_re * a2_i[:, None, :] + c_im * a2_r[:, None, :]
    k0 = (jnp.einsum('gon,gni->gio', c_re, bb_r, precision=hi)
          - jnp.einsum('gon,gni->gio', c_im, bb_i, precision=hi))
    k1 = (jnp.einsum('gon,gni->gio', c_re, abb_r, precision=hi)
          - jnp.einsum('gon,gni->gio', c_im, abb_i, precision=hi))

    gb = SSM_BLOCK_GROUPS
    n_blocks = n_groups // gb
    eye = jnp.eye(gb, dtype=F32)

    def diag(w):
        p, q = w.shape[1:]
        w = w.reshape(n_blocks, gb, p, q)
        return (w[:, :, :, None, :] * eye[None, :, None, :, None]).reshape(n_blocks, gb * p, gb * q)

    t = lambda w: jnp.swapaxes(w, 1, 2)
    wd = jnp.concatenate([
        jnp.concatenate([diag(t(abb_r)), diag(t(abb_i))], axis=2),
        jnp.concatenate([diag(t(bb_r)), diag(t(bb_i))], axis=2),
    ], axis=1).astype(BF16)
    wr = jnp.concatenate([
        jnp.concatenate([diag(t(ca_r)), diag(t(-ca_i))], axis=1),
        jnp.concatenate([diag(t(ca2_r)), diag(t(-ca2_i))], axis=1),
    ], axis=2).astype(BF16)
    dk0, dk1 = diag(k0), diag(k1)
    wx = jnp.concatenate([
        jnp.concatenate([dk0, dk1], axis=2),
        jnp.concatenate([jnp.zeros_like(dk0), dk0], axis=2),
    ], axis=1).astype(BF16)
    return (wd, wr, wx, a2_r.reshape(n_blocks, 1, gb * n_state), a2_i.reshape(n_blocks, 1, gb * n_state))


def _token_perm(nb_batch, steps):
    dst = jnp.arange(nb_batch * steps)
    b, t = dst // steps, dst % steps
    src = (t % 2) * (steps // 2) * nb_batch + (t // 2) * nb_batch + b
    return (src[:, None] == jnp.arange(nb_batch * steps)[None, :]).astype(BF16)


def _tiles(m, seq):
    tm = min(512, seq)
    tc = min(64, seq)
    tm_ffn = min(1024, seq)
    rsub = min(512, tm_ffn)
    return tm, tc, tm_ffn, rsub


def kernel(x, p, mix_norm, w_in, sgu_norm, sgu_w, sgu_b, s5_lam_re, s5_lam_im, s5_log_dt,
           s5_b_re, s5_b_im, s5_c_re, s5_c_im, s5_d, s5_glu_w, s5_glu_b,
           out_norm_a, out_norm_b, w_out, ffn_norm, ffn_w_up, ffn_conv_w, ffn_conv_b,
           ffn_w_down, ple_norm, ple_w_gate, ple_w_proj, final_norm):
    depth = w_in.shape[0]
    nb_batch, seq, d = x.shape
    m = nb_batch * seq
    d_sgu = sgu_norm.shape[-1]
    d_ssm = s5_d.shape[-1]
    d_ff = ffn_w_down.shape[1]
    n_heads = d_sgu // SGU_HEAD
    assert seq % SGU_CHUNK == 0 and d_sgu % SGU_HEAD == 0
    assert s5_lam_re.shape[1] % SSM_BLOCK_GROUPS == 0
    tm, tc, tm_ffn, rsub = _tiles(m, seq)
    tf = 512 if d_ff % 512 == 0 else d_ff
    assert seq % tm == 0 and tm % SGU_CHUNK == 0 and seq % tc == 0 and tc % 32 == 0
    assert seq % tm_ffn == 0 and tm_ffn % rsub == 0 and rsub % CARRY == 0

    permt = _token_perm(nb_batch, tc // 2)
    tril = jnp.tril(jnp.ones((SGU_CHUNK, SGU_CHUNK), F32))
    row = lambda v: v.reshape(1, -1)

    w_in_b, w_out_b, glu_b16 = w_in.astype(BF16), w_out.astype(BF16), s5_glu_w.astype(BF16)
    w_up_b, w_down_b = ffn_w_up.astype(BF16), ffn_w_down.astype(BF16)
    w_gate_b, w_proj_b = ple_w_gate.astype(BF16), ple_w_proj.astype(BF16)

    x2 = x.reshape(m, d)
    p3 = p.reshape(depth, m, -1)
    for i in range(depth):
        ws = (sgu_w[i] * tril[None]).astype(BF16)
        bs_full = jnp.repeat(jnp.transpose(sgu_b[i]), SGU_HEAD, axis=1)
        ya, xs = _mix_in(x2, row(mix_norm[i]), w_in_b, row(sgu_norm[i]), ws, bs_full,
                         row(out_norm_a[i]), layer=i, tm=tm, tc=tc, seq=seq)
        s5w = _s5_params(s5_lam_re[i], s5_lam_im[i], s5_log_dt[i], s5_b_re[i], s5_b_im[i],
                         s5_c_re[i], s5_c_im[i])
        yb = _s5(xs, permt, *s5w, row(s5_d[i]), glu_b16, row(s5_glu_b[i]),
                 row(out_norm_b[i]), layer=i, tc=tc, nb_batch=nb_batch)
        x2, hn = _mix_out(x2, ya, yb.reshape(m, d_ssm), w_out_b, row(ffn_norm[i]), layer=i, tm=tm)
        delta = _ffn(hn, w_up_b, ffn_conv_w[i], row(ffn_conv_b[i]), w_down_b,
                     layer=i, tm=tm_ffn, tf=tf, rsub=rsub, seq=seq)
        x2 = _ple(x2, delta, p3, row(ple_norm[i]), w_gate_b, w_proj_b, row(final_norm),
                  layer=i, tm=tm, final=(i == depth - 1))
    return x2.reshape(nb_batch, seq, d)
```

```python
import functools
import math

import jax
import jax.numpy as jnp
from jax import lax
from jax.experimental import pallas as pl
from jax.experimental.pallas import tpu as pltpu

EPS = 1e-6
SGU_CHUNK = 128
SGU_HEAD = 128
SSM_GROUP = 16
CONV_W = 3
LANES = 128
SSM_BLOCK_GROUPS = LANES // SSM_GROUP
GATHER_PAD = 8
CARRY = 8
V7X_VMEM_LIMIT_BYTES = 56 * 1024 * 1024

F32 = jnp.float32
BF16 = jnp.bfloat16


def _dot(a, b):
    return jnp.dot(a, b, preferred_element_type=F32)


def _rms(x, g):
    return x * lax.rsqrt(jnp.mean(x * x, axis=-1, keepdims=True) + EPS) * g


def _resident(shape):
    nd = len(shape)
    return pl.BlockSpec(shape, lambda *_: (0,) * nd, pipeline_mode=pl.Buffered(1))


def _layer(shape, layer, block=0):
    nd = len(shape)
    return pl.BlockSpec((None,) + tuple(shape), lambda *_: (layer, block) + (0,) * (nd - 1),
                        pipeline_mode=pl.Buffered(1))


def _params(semantics):
    return pltpu.CompilerParams(dimension_semantics=semantics,
                                vmem_limit_bytes=V7X_VMEM_LIMIT_BYTES)


def _mix_in_kernel(x_ref, gmix_ref, win_ref, gsgu_ref, ws_ref, bs_ref, ga_ref,
                   ya_ref, xs_ref, u_ref, vn_ref, *, d_sgu, n_heads):
    tm = x_ref.shape[0]
    h = _rms(x_ref[...], gmix_ref[...]).astype(BF16)
    u_ref[...] = jax.nn.gelu(_dot(h, win_ref[:, :d_sgu]))
    v = jax.nn.gelu(_dot(h, win_ref[:, d_sgu:2 * d_sgu]))
    xs = _dot(h, win_ref[:, 2 * d_sgu:])
    n_tt, n_slabs, pitch, _ = xs_ref.shape
    tc = pitch - GATHER_PAD
    for j in range(n_slabs):
        xs_ref[:, j, pl.ds(0, tc), :] = xs[:, j * LANES:(j + 1) * LANES].reshape(n_tt, tc, LANES)
        xs_ref[:, j, pl.ds(tc, GATHER_PAD), :] = jnp.zeros((n_tt, GATHER_PAD, LANES), F32)
    vc = v - jnp.mean(v, axis=-1, keepdims=True)
    vn = vc * lax.rsqrt(jnp.mean(vc * vc, axis=-1, keepdims=True) + EPS) * gsgu_ref[...]
    vn = vn.astype(BF16)
    n_pairs = tm // (2 * SGU_CHUNK)
    for k in range(n_pairs):
        for e in range(2):
            src = slice((2 * k + e) * SGU_CHUNK, (2 * k + e + 1) * SGU_CHUNK)
            for hd in range(n_heads):
                vn_ref[pl.ds(k * SGU_CHUNK, SGU_CHUNK), pl.ds((2 * hd + e) * SGU_HEAD, SGU_HEAD)] = (
                    vn[src, hd * SGU_HEAD:(hd + 1) * SGU_HEAD])
    for k in range(n_pairs):
        for hd in range(n_heads):
            cols = pl.ds(hd * SGU_HEAD, SGU_HEAD)
            sv = _dot(ws_ref[hd], vn_ref[pl.ds(k * SGU_CHUNK, SGU_CHUNK), pl.ds(2 * hd * SGU_HEAD, 2 * SGU_HEAD)])
            for e in range(2):
                rows = pl.ds((2 * k + e) * SGU_CHUNK, SGU_CHUNK)
                u_ref[rows, cols] = u_ref[rows, cols] * (sv[:, e * SGU_HEAD:(e + 1) * SGU_HEAD] + bs_ref[:, cols])
    ya_ref[...] = _rms(u_ref[...], ga_ref[...]).astype(BF16)


def _mix_in(x2, gmix, win, gsgu, ws, bs_full, ga, *, layer, tm, tc, seq):
    m, d = x2.shape
    tiles_per_seq = seq // tm
    d_sgu = gsgu.shape[-1]
    n_slabs = (win.shape[2] - 2 * d_sgu) // LANES
    n_heads = d_sgu // SGU_HEAD
    kern = functools.partial(_mix_in_kernel, d_sgu=d_sgu, n_heads=n_heads)
    return pl.pallas_call(
        kern,
        grid=(m // tm,),
        in_specs=[
            pl.BlockSpec((tm, d), lambda i: (i, 0)),
            _resident((1, d)),
            _layer(win.shape[1:], layer),
            _resident((1, d_sgu)),
            _resident(ws.shape),
            _resident(bs_full.shape),
            _resident((1, d_sgu)),
        ],
        out_specs=[
            pl.BlockSpec((tm, d_sgu), lambda i: (i, 0)),
            pl.BlockSpec((tm // tc, n_slabs, tc + GATHER_PAD, LANES),
                         lambda i: (i % tiles_per_seq, 0, i // tiles_per_seq, 0)),
        ],
        out_shape=[
            jax.ShapeDtypeStruct((m, d_sgu), BF16),
            jax.ShapeDtypeStruct((seq // tc, n_slabs, (m // seq) * (tc + GATHER_PAD), LANES), F32),
        ],
        scratch_shapes=[
            pltpu.VMEM((tm, d_sgu), F32),
            pltpu.VMEM((tm // 2, 2 * d_sgu), BF16),
        ],
        compiler_params=_params(("parallel",)),
        name="mix_in",
    )(x2, gmix, win, gsgu, ws, bs_full, ga)


def _s5_kernel(xs_ref, permt_ref, wd_ref, wr_ref, wx_ref, a2r_ref, a2i_ref, d_ref,
               gluw_ref, glub_ref, gb_ref, out_ref,
               h_ref, dr_ref, sp_ref, xtok_ref, xpair_ref, ytok_ref, *, lane_w):
    n_slabs, prow, _ = xs_ref.shape
    nb_batch, tc, d_ssm = out_ref.shape
    pitch = prow // nb_batch
    n_blocks, _, two_n = wd_ref.shape
    n_state = two_n // 2
    n_pairs = tc // 2
    hp = n_pairs // 2
    hrows = hp * nb_batch
    pair_lanes = 2 * LANES

    @pl.when(pl.program_id(0) == 0)
    def _():
        h_ref[...] = jnp.zeros_like(h_ref)

    def tok_rows(half, parity):
        return pl.ds((2 * half + parity) * hrows, hrows)

    for j in range(n_slabs):
        for t in range(tc):
            tau, parity = divmod(t, 2)
            half, tau_h = divmod(tau, hp)
            tile = xs_ref[j, pl.ds(t, nb_batch, stride=pitch), :]
            xtok_ref[pl.ds((2 * half + parity) * hrows + tau_h * nb_batch, nb_batch),
                     pl.ds(j * LANES, LANES)] = tile
            xpair_ref[pl.ds(tau * nb_batch, nb_batch),
                      pl.ds((2 * j + parity) * LANES, LANES)] = tile.astype(BF16)

    def drive(blk):
        dr_ref[blk % 2] = _dot(xpair_ref[:, pl.ds(blk * pair_lanes, pair_lanes)], wd_ref[blk])

    def scan(blk):
        buf = dr_ref.at[blk % 2]
        sp = sp_ref.at[blk % 2]
        for q in range(n_state // lane_w):
            re = pl.ds(q * lane_w, lane_w)
            im = pl.ds(n_state + q * lane_w, lane_w)
            a_r = jnp.broadcast_to(a2r_ref[blk, :, re], (nb_batch, lane_w))
            a_i = jnp.broadcast_to(a2i_ref[blk, :, re], (nb_batch, lane_w))
            h_r, h_i = h_ref[blk, :, re], h_ref[blk, :, im]
            for tau in range(n_pairs):
                r = pl.ds(tau * nb_batch, nb_batch)
                sp[r, re] = h_r.astype(BF16)
                sp[r, im] = h_i.astype(BF16)
                h_r, h_i = (a_r * h_r - a_i * h_i + buf[r, re],
                            a_r * h_i + a_i * h_r + buf[r, im])
            h_ref[blk, :, re] = h_r
            h_ref[blk, :, im] = h_i

    def readout(blk):
        yy = (_dot(sp_ref[blk % 2], wr_ref[blk])
              + _dot(xpair_ref[:, pl.ds(blk * pair_lanes, pair_lanes)], wx_ref[blk]))
        ch = pl.ds(blk * LANES, LANES)
        for half in range(2):
            for parity in range(2):
                dst = tok_rows(half, parity)
                ytok_ref[dst, ch] = (yy[half * hrows:(half + 1) * hrows, parity * LANES:(parity + 1) * LANES]
                                     + d_ref[:, ch] * xtok_ref[dst, ch])

    drive(0)
    for blk in range(n_blocks):
        if blk + 1 < n_blocks:
            drive(blk + 1)
        scan(blk)
        readout(blk)

    z = jax.nn.gelu(ytok_ref[...])
    z = z * jax.nn.sigmoid(_dot(z.astype(BF16), gluw_ref[...]) + glub_ref[...])
    zn = _rms(z, gb_ref[...]).astype(BF16)
    for half in range(2):
        out_ref[:, pl.ds(half * (tc // 2), tc // 2), :] = (
            _dot(permt_ref[...], zn[2 * half * hrows:2 * (half + 1) * hrows])
            .astype(BF16).reshape(nb_batch, tc // 2, d_ssm))


def _s5(xs4, permt, wd, wr, wx, a2_r, a2_i, dskip, gluw, glub, gb, *, layer, tc, nb_batch):
    n_tiles, n_slabs, prow, _ = xs4.shape
    d_ssm = n_slabs * LANES
    seq = n_tiles * tc
    rows = nb_batch * tc
    n_blocks, _, two_n = wd.shape
    assert n_blocks == n_slabs and prow == nb_batch * (tc + GATHER_PAD)
    kern = functools.partial(_s5_kernel, lane_w=256)
    return pl.pallas_call(
        kern,
        grid=(n_tiles,),
        in_specs=[
            pl.BlockSpec((None, n_slabs, prow, LANES), lambda i: (i, 0, 0, 0)),
            _resident(permt.shape),
            _resident(wd.shape),
            _resident(wr.shape),
            _resident(wx.shape),
            _resident(a2_r.shape),
            _resident(a2_i.shape),
            _resident((1, d_ssm)),
            _layer(gluw.shape[1:], layer),
            _resident((1, d_ssm)),
            _resident((1, d_ssm)),
        ],
        out_specs=pl.BlockSpec((nb_batch, tc, d_ssm), lambda i: (0, i, 0)),
        out_shape=jax.ShapeDtypeStruct((nb_batch, seq, d_ssm), BF16),
        scratch_shapes=[
            pltpu.VMEM((n_blocks, nb_batch, two_n), F32),
            pltpu.VMEM((2, rows // 2, two_n), F32),
            pltpu.VMEM((2, rows // 2, two_n), BF16),
            pltpu.VMEM((rows, d_ssm), F32),
            pltpu.VMEM((rows // 2, 2 * d_ssm), BF16),
            pltpu.VMEM((rows, d_ssm), F32),
        ],
        compiler_params=_params(("arbitrary",)),
        name="s5",
    )(xs4, permt, wd, wr, wx, a2_r, a2_i, dskip, gluw, glub, gb)


def _mix_out_kernel(x_ref, ya_ref, yb_ref, wa_ref, wb_ref, g_ref, o_ref, hn_ref):
    x1 = x_ref[...] + _dot(ya_ref[...], wa_ref[...]) + _dot(yb_ref[...], wb_ref[...])
    o_ref[...] = x1
    hn_ref[...] = _rms(x1, g_ref[...]).astype(BF16)


def _mix_out(x2, ya, yb, wo, g, *, layer, tm):
    m, d = x2.shape
    return pl.pallas_call(
        _mix_out_kernel,
        grid=(m // tm,),
        in_specs=[
            pl.BlockSpec((tm, d), lambda i: (i, 0)),
            pl.BlockSpec((tm, ya.shape[1]), lambda i: (i, 0)),
            pl.BlockSpec((tm, yb.shape[1]), lambda i: (i, 0)),
            _layer((ya.shape[1], d), layer, 0),
            _layer((yb.shape[1], d), layer, 1),
            _resident((1, d)),
        ],
        out_specs=[
            pl.BlockSpec((tm, d), lambda i: (i, 0)),
            pl.BlockSpec((tm, d), lambda i: (i, 0)),
        ],
        out_shape=[
            jax.ShapeDtypeStruct((m, d), F32),
            jax.ShapeDtypeStruct((m, d), BF16),
        ],
        compiler_params=_params(("parallel",)),
        name="mix_out",
    )(x2, ya, yb, wo, wo, g)


def _ffn_kernel(hn_ref, wg_ref, wu_ref, cwg_ref, cwu_ref, cbg_ref, cbu_ref, wd_ref, o_ref,
                carry_ref, a_ref, *, tiles_per_seq, rsub):
    i = pl.program_id(0)
    f = pl.program_id(1)
    tm = hn_ref.shape[0]
    tf = wd_ref.shape[0]
    n_sub = tm // rsub

    @pl.when(f == 0)
    def _():
        o_ref[...] = jnp.zeros_like(o_ref)

    @pl.when(i % tiles_per_seq == 0)
    def _():
        carry_ref[f] = jnp.zeros(carry_ref.shape[1:], F32)

    a_ref[0, pl.ds(0, CARRY), :] = carry_ref[f]

    def project(r):
        h = hn_ref[pl.ds(r * rsub, rsub), :]
        a_ref[r % 2, pl.ds(CARRY, rsub), pl.ds(0, tf)] = _dot(h, wg_ref[...])
        a_ref[r % 2, pl.ds(CARRY, rsub), pl.ds(tf, tf)] = _dot(h, wu_ref[...])

    def conv(slot, cols, cw_ref, cb_ref):
        return (cw_ref[0:1, :] * a_ref[slot, pl.ds(CARRY - 2, rsub), cols]
                + cw_ref[1:2, :] * a_ref[slot, pl.ds(CARRY - 1, rsub), cols]
                + cw_ref[2:3, :] * a_ref[slot, pl.ds(CARRY, rsub), cols] + cb_ref[...])

    project(0)
    for r in range(n_sub):
        slot = r % 2
        if r + 1 < n_sub:
            project(r + 1)
        gate = conv(slot, pl.ds(0, tf), cwg_ref, cbg_ref)
        up = conv(slot, pl.ds(tf, tf), cwu_ref, cbu_ref)
        tail = a_ref[slot, pl.ds(rsub, CARRY), :]
        if r + 1 < n_sub:
            a_ref[1 - slot, pl.ds(0, CARRY), :] = tail
        else:
            carry_ref[f] = tail
        act = (jax.nn.silu(gate) * up).astype(BF16)
        o_ref[pl.ds(r * rsub, rsub), :] += _dot(act, wd_ref[...])


def _ffn(hn, w_up, conv_w, conv_b, w_down, *, layer, tm, tf, rsub, seq):
    m, d = hn.shape
    d_ff = w_down.shape[1]
    nf = d_ff // tf
    kern = functools.partial(_ffn_kernel, tiles_per_seq=seq // tm, rsub=rsub)
    return pl.pallas_call(
        kern,
        grid=(m // tm, nf),
        in_specs=[
            pl.BlockSpec((tm, d), lambda i, f: (i, 0)),
            pl.BlockSpec((None, d, tf), lambda i, f: (layer, 0, f)),
            pl.BlockSpec((None, d, tf), lambda i, f: (layer, 0, nf + f)),
            pl.BlockSpec((CONV_W, tf), lambda i, f: (0, f)),
            pl.BlockSpec((CONV_W, tf), lambda i, f: (0, nf + f)),
            pl.BlockSpec((1, tf), lambda i, f: (0, f)),
            pl.BlockSpec((1, tf), lambda i, f: (0, nf + f)),
            pl.BlockSpec((None, tf, d), lambda i, f: (layer, f, 0)),
        ],
        out_specs=pl.BlockSpec((tm, d), lambda i, f: (i, 0)),
        out_shape=jax.ShapeDtypeStruct((m, d), F32),
        scratch_shapes=[
            pltpu.VMEM((nf, CARRY, 2 * tf), F32),
            pltpu.VMEM((2, CARRY + rsub, 2 * tf), F32),
        ],
        compiler_params=_params(("arbitrary", "arbitrary")),
        name="ffn",
    )(hn, w_up, w_up, conv_w, conv_w, conv_b, conv_b, w_down)


def _ple_kernel(x_ref, dl_ref, p_ref, g_ref, wg_ref, wp_ref, gf_ref, o_ref, *, final):
    x = x_ref[...] + dl_ref[...]
    gate = jax.nn.sigmoid(_dot(_rms(x, g_ref[...]).astype(BF16), wg_ref[...]))
    y = x + gate * _dot(p_ref[...].astype(BF16), wp_ref[...])
    if final:
        y = _rms(y, gf_ref[...])
    o_ref[...] = y


def _ple(x2, delta, p3, g, wg, wp, gf, *, layer, tm, final):
    m, d = x2.shape
    kern = functools.partial(_ple_kernel, final=final)
    return pl.pallas_call(
        kern,
        grid=(m // tm,),
        in_specs=[
            pl.BlockSpec((tm, d), lambda i: (i, 0)),
            pl.BlockSpec((tm, d), lambda i: (i, 0)),
            pl.BlockSpec((None, tm, p3.shape[2]), lambda i: (layer, i, 0)),
            _resident((1, d)),
            _layer(wg.shape[1:], layer),
            _layer(wp.shape[1:], layer),
            _resident((1, d)),
        ],
        out_specs=pl.BlockSpec((tm, d), lambda i: (i, 0)),
        out_shape=jax.ShapeDtypeStruct((m, d), F32),
        compiler_params=_params(("parallel",)),
        name="ple_final" if final else "ple",
    )(x2, delta, p3, g, wg, wp, gf)


def _s5_params(lam_re, lam_im, log_dt, b_re, b_im, c_re, c_im):
    n_groups, n_state = lam_re.shape
    hi = lax.Precision.HIGHEST
    dt = jnp.exp(log_dt)[:, None]
    mag = jnp.exp(lam_re * dt)
    ab_r = mag * jnp.cos(lam_im * dt)
    ab_i = mag * jnp.sin(lam_im * dt)
    den = lam_re * lam_re + lam_im * lam_im
    nr = ab_r - 1.0
    q_r = (nr * lam_re + ab_i * lam_im) / den
    q_i = (ab_i * lam_re - nr * lam_im) / den
    bb_r = q_r[..., None] * b_re - q_i[..., None] * b_im
    bb_i = q_r[..., None] * b_im + q_i[..., None] * b_re
    abb_r = ab_r[..., None] * bb_r - ab_i[..., None] * bb_i
    abb_i = ab_r[..., None] * bb_i + ab_i[..., None] * bb_r
    a2_r = ab_r * ab_r - ab_i * ab_i
    a2_i = 2.0 * ab_r * ab_i
    ca_r = c_re * ab_r[:, None, :] - c_im * ab_i[:, None, :]
    ca_i = c_re * ab_i[:, None, :] + c_im * ab_r[:, None, :]
    ca2_r = c_re * a2_r[:, None, :] - c_im * a2_i[:, None, :]
    ca2_i = c_re * a2_i[:, None, :] + c_im * a2_r[:, None, :]
    k0 = (jnp.einsum('gon,gni->gio', c_re, bb_r, precision=hi)
          - jnp.einsum('gon,gni->gio', c_im, bb_i, precision=hi))
    k1 = (jnp.einsum('gon,gni->gio', c_re, abb_r, precision=hi)
          - jnp.einsum('gon,gni->gio', c_im, abb_i, precision=hi))

    gb = SSM_BLOCK_GROUPS
    n_blocks = n_groups // gb
    eye = jnp.eye(gb, dtype=F32)

    def diag(w):
        p, q = w.shape[1:]
        w = w.reshape(n_blocks, gb, p, q)
        return (w[:, :, :, None, :] * eye[None, :, None, :, None]).reshape(n_blocks, gb * p, gb * q)

    t = lambda w: jnp.swapaxes(w, 1, 2)
    wd = jnp.concatenate([
        jnp.concatenate([diag(t(abb_r)), diag(t(abb_i))], axis=2),
        jnp.concatenate([diag(t(bb_r)), diag(t(bb_i))], axis=2),
    ], axis=1).astype(BF16)
    wr = jnp.concatenate([
        jnp.concatenate([diag(t(ca_r)), diag(t(-ca_i))], axis=1),
        jnp.concatenate([diag(t(ca2_r)), diag(t(-ca2_i))], axis=1),
    ], axis=2).astype(BF16)
    dk0, dk1 = diag(k0), diag(k1)
    wx = jnp.concatenate([
        jnp.concatenate([dk0, dk1], axis=2),
        jnp.concatenate([jnp.zeros_like(dk0), dk0], axis=2),
    ], axis=1).astype(BF16)
    return (wd, wr, wx, a2_r.reshape(n_blocks, 1, gb * n_state), a2_i.reshape(n_blocks, 1, gb * n_state))


def _token_perm(nb_batch, steps):
    dst = jnp.arange(nb_batch * steps)
    b, t = dst // steps, dst % steps
    src = (t % 2) * (steps // 2) * nb_batch + (t // 2) * nb_batch + b
    return (src[:, None] == jnp.arange(nb_batch * steps)[None, :]).astype(BF16)


def _tiles(m, seq):
    tm = min(512, seq)
    tc = min(64, seq)
    tm_ffn = min(1024, seq)
    rsub = min(512, tm_ffn)
    return tm, tc, tm_ffn, rsub


def kernel(x, p, mix_norm, w_in, sgu_norm, sgu_w, sgu_b, s5_lam_re, s5_lam_im, s5_log_dt,
           s5_b_re, s5_b_im, s5_c_re, s5_c_im, s5_d, s5_glu_w, s5_glu_b,
           out_norm_a, out_norm_b, w_out, ffn_norm, ffn_w_up, ffn_conv_w, ffn_conv_b,
           ffn_w_down, ple_norm, ple_w_gate, ple_w_proj, final_norm):
    depth = w_in.shape[0]
    nb_batch, seq, d = x.shape
    m = nb_batch * seq
    d_sgu = sgu_norm.shape[-1]
    d_ssm = s5_d.shape[-1]
    d_ff = ffn_w_down.shape[1]
    n_heads = d_sgu // SGU_HEAD
    assert seq % SGU_CHUNK == 0 and d_sgu % SGU_HEAD == 0
    assert s5_lam_re.shape[1] % SSM_BLOCK_GROUPS == 0
    tm, tc, tm_ffn, rsub = _tiles(m, seq)
    tf = 512 if d_ff % 512 == 0 else d_ff
    assert seq % tm == 0 and tm % (2 * SGU_CHUNK) == 0 and seq % tc == 0 and tc % 32 == 0
    assert seq % tm_ffn == 0 and tm_ffn % rsub == 0 and rsub % CARRY == 0

    permt = _token_perm(nb_batch, tc // 2)
    tril = jnp.tril(jnp.ones((SGU_CHUNK, SGU_CHUNK), F32))
    row = lambda v: v.reshape(1, -1)

    w_in_b, w_out_b, glu_b16 = w_in.astype(BF16), w_out.astype(BF16), s5_glu_w.astype(BF16)
    w_up_b, w_down_b = ffn_w_up.astype(BF16), ffn_w_down.astype(BF16)
    w_gate_b, w_proj_b = ple_w_gate.astype(BF16), ple_w_proj.astype(BF16)

    x2 = x.reshape(m, d)
    p3 = p.reshape(depth, m, -1)
    for i in range(depth):
        ws = (sgu_w[i] * tril[None]).astype(BF16)
        bs_full = jnp.repeat(jnp.transpose(sgu_b[i]), SGU_HEAD, axis=1)
        ya, xs = _mix_in(x2, row(mix_norm[i]), w_in_b, row(sgu_norm[i]), ws, bs_full,
                         row(out_norm_a[i]), layer=i, tm=tm, tc=tc, seq=seq)
        s5w = _s5_params(s5_lam_re[i], s5_lam_im[i], s5_log_dt[i], s5_b_re[i], s5_b_im[i],
                         s5_c_re[i], s5_c_im[i])
        yb = _s5(xs, permt, *s5w, row(s5_d[i]), glu_b16, row(s5_glu_b[i]),
                 row(out_norm_b[i]), layer=i, tc=tc, nb_batch=nb_batch)
        x2, hn = _mix_out(x2, ya, yb.reshape(m, d_ssm), w_out_b, row(ffn_norm[i]), layer=i, tm=tm)
        delta = _ffn(hn, w_up_b, ffn_conv_w[i], row(ffn_conv_b[i]), w_down_b,
                     layer=i, tm=tm_ffn, tf=tf, rsub=rsub, seq=seq)
        x2 = _ple(x2, delta, p3, row(ple_norm[i]), w_gate_b, w_proj_b, row(final_norm),
                  layer=i, tm=tm, final=(i == depth - 1))
    return x2.reshape(nb_batch, seq, d)
```
